```python
import math
import jax, jax.numpy as jnp
from jax import lax
import numpy as np

D_MODEL = 1024
BATCH = 2
SEQ = 8192
DEPTH = 4
DEC_BATCH = 128
DEC_SEQ = 1
PAST_LEN = 2048
PAGE_SIZE = 128

CONV_DIM = D_MODEL // 2
CONV_GROUPS = 8
CONV_W = 3
ATT_DIM = D_MODEL // 2
HEAD_DIM = 64
N_HEADS = ATT_DIM // HEAD_DIM
IDX_HEADS = 8
IDX_DIM = 64
TOPK_MAX = 256
N_BUCKETS = 32
REL_MAX_DIST = 128
D_FF = ((8 * D_MODEL // 3 + 127) // 128) * 128
Q_BLOCK = 128
EPS = 1e-6
NEG_INF = -1e30
N_IN = 3 * CONV_DIM + 3 * ATT_DIM + IDX_HEADS * IDX_DIM + IDX_DIM + IDX_HEADS + 2 * D_MODEL

kernel_name = "hybrid_gatedconv_dsa_macaron_step"


def _rmsnorm(x, g):
    xf = x.astype(jnp.float32)
    y = xf * lax.rsqrt(jnp.mean(xf * xf, axis=-1, keepdims=True) + EPS)
    return (y * g.astype(jnp.float32)).astype(x.dtype)


def _swiglu(x, w_in, w_down):
    g, u = jnp.split(x @ w_in, 2, axis=-1)
    return (jax.nn.silu(g) * u) @ w_down


def _split_proj(p):
    sizes = (CONV_DIM, CONV_DIM, CONV_DIM, ATT_DIM, ATT_DIM, ATT_DIM,
             IDX_HEADS * IDX_DIM, IDX_DIM, IDX_HEADS, D_MODEL, D_MODEL)
    cuts, acc = [], 0
    for s in sizes[:-1]:
        acc += s
        cuts.append(acc)
    return jnp.split(p, cuts, axis=-1)


def _short_conv(z_full, w):
    t = z_full.shape[1] - (CONV_W - 1)
    out = w[0] * z_full[:, 0:t]
    for j in range(1, CONV_W):
        out = out + w[j] * z_full[:, j:j + t]
    return out


def _t5_bucket(dist):
    n = jnp.maximum(dist, 0)
    max_exact = N_BUCKETS // 2
    nf = jnp.maximum(n, 1).astype(jnp.float32)
    large = max_exact + (jnp.log(nf / max_exact) / math.log(REL_MAX_DIST / max_exact)
                         * (N_BUCKETS - max_exact)).astype(jnp.int32)
    large = jnp.minimum(large, N_BUCKETS - 1)
    return jnp.where(n < max_exact, n, large)


def _index_scores(q_idx, w_idx, k_idx):
    dots = jnp.einsum('bqhd,bsd->bqsh', q_idx.astype(jnp.float32), k_idx.astype(jnp.float32)) * IDX_DIM ** -0.5
    return jnp.einsum('bqsh,bqh->bqs', jax.nn.relu(dots), w_idx.astype(jnp.float32))


def _attend(q, k_sel, v_sel, sel, q_pos, rel_bias):
    dist = q_pos[None, :, None] - sel
    valid = dist >= 0
    bias = rel_bias[_t5_bucket(dist)]
    logits = jnp.einsum('bqhd,bqkhd->bqhk', q, k_sel).astype(jnp.float32) * HEAD_DIM ** -0.5
    logits = logits + jnp.swapaxes(bias, -1, -2).astype(jnp.float32)
    logits = jnp.where(valid[:, :, None, :], logits, NEG_INF)
    p = jax.nn.softmax(logits, axis=-1)
    return jnp.einsum('bqhk,bqkhd->bqhd', p.astype(v_sel.dtype), v_sel)


def _prompt_attention(q, k, v, q_idx, w_idx, k_idx, rel_bias):
    b, s = q.shape[0], q.shape[1]
    top_k = min(TOPK_MAX, s // 4)
    n_blk = s // Q_BLOCK
    key_pos = jnp.arange(s, dtype=jnp.int32)
    bidx = jnp.arange(b)[:, None, None]

    def block(i):
        start = i * Q_BLOCK
        qb = lax.dynamic_slice_in_dim(q, start, Q_BLOCK, axis=1)
        qib = lax.dynamic_slice_in_dim(q_idx, start, Q_BLOCK, axis=1)
        wib = lax.dynamic_slice_in_dim(w_idx, start, Q_BLOCK, axis=1)
        q_pos = start + jnp.arange(Q_BLOCK, dtype=jnp.int32)
        scores = _index_scores(qib, wib, k_idx)
        causal = key_pos[None, :] <= q_pos[:, None]
        scores = jnp.where(causal[None], scores, NEG_INF)
        _, sel = lax.top_k(scores, top_k)
        return _attend(qb, k[bidx, sel], v[bidx, sel], sel, q_pos, rel_bias)

    out = lax.map(block, jnp.arange(n_blk, dtype=jnp.int32))
    return jnp.transpose(out, (1, 0, 2, 3, 4)).reshape(b, s, N_HEADS, HEAD_DIM)


def _sample_attention(l, q, k_new, v_new, q_idx, w_idx, k_idx_new,
                      cache_k, cache_v, cache_idx_k, page_table, rel_bias):
    db, t = q.shape[0], q.shape[1]
    n_pages = page_table.shape[1]
    past = n_pages * PAGE_SIZE
    top_k = min(TOPK_MAX, (past + t) // 4)
    bidx = jnp.arange(db)[:, None, None]
    ik_past = cache_idx_k[l, page_table].reshape(db, past, IDX_DIM)
    ik_all = jnp.concatenate([ik_past, k_idx_new.astype(ik_past.dtype)], axis=1)
    q_pos = past + jnp.arange(t, dtype=jnp.int32)
    scores = _index_scores(q_idx, w_idx, ik_all)
    causal = jnp.arange(past + t, dtype=jnp.int32)[None, :] <= q_pos[:, None]
    scores = jnp.where(causal[None], scores, NEG_INF)
    _, sel = lax.top_k(scores, top_k)
    is_past = (sel < past)[..., None, None]
    sp = jnp.minimum(sel, past - 1)
    phys = page_table[bidx, sp // PAGE_SIZE]
    off = sp % PAGE_SIZE
    sn = jnp.clip(sel - past, 0, t - 1)
    k_sel = jnp.where(is_past, cache_k[l, phys, off], k_new[bidx, sn].astype(cache_k.dtype))
    v_sel = jnp.where(is_past, cache_v[l, phys, off], v_new[bidx, sn].astype(cache_v.dtype))
    return _attend(q, k_sel.astype(q.dtype), v_sel.astype(q.dtype), sel, q_pos, rel_bias)


def _trunk(x, conv_prev, attend, norm_ffn1, w_ffn1_in, w_ffn1_out, norm_mix, w_in, conv_w,
           w_o_conv, w_o_attn, w_out, norm_ffn2, w_ffn2_in, w_ffn2_out, norm_final):
    b, t, _ = x.shape
    ks, vs, iks, convs = [], [], [], []
    for l in range(DEPTH):
        h = x + 0.5 * _swiglu(_rmsnorm(x, norm_ffn1[l]), w_ffn1_in[l], w_ffn1_out[l])
        u = _rmsnorm(h, norm_mix[l])
        xc, bc, cc, q, k, v, qi, ki, wi, gc, ga = _split_proj(u @ w_in[l])
        z_full = jnp.concatenate([conv_prev[l].astype(xc.dtype), cc * xc], axis=1)
        y_conv = (bc * _short_conv(z_full, conv_w[l])) @ w_o_conv[l]
        q = q.reshape(b, t, N_HEADS, HEAD_DIM)
        k = k.reshape(b, t, N_HEADS, HEAD_DIM)
        v = v.reshape(b, t, N_HEADS, HEAD_DIM)
        qi = qi.reshape(b, t, IDX_HEADS, IDX_DIM)
        wi = wi * IDX_HEADS ** -0.5
        o = attend(l, q, k, v, qi, wi, ki)
        y_attn = o.reshape(b, t, ATT_DIM) @ w_o_attn[l]
        merged = jax.nn.sigmoid(gc) * y_conv + jax.nn.sigmoid(ga) * y_attn
        x = h + merged @ w_out[l]
        x = x + 0.5 * _swiglu(_rmsnorm(x, norm_ffn2[l]), w_ffn2_in[l], w_ffn2_out[l])
        ks.append(k)
        vs.append(v)
        iks.append(ki)
        convs.append(z_full[:, -(CONV_W - 1):])
    return (_rmsnorm(x, norm_final), jnp.stack(ks), jnp.stack(vs), jnp.stack(iks), jnp.stack(convs))


def setup_inputs(seed: int = 0) -> dict:
    key = jax.random.key(seed)
    ks = jax.random.split(key, 24)
    n_pages = PAST_LEN // PAGE_SIZE
    n_used = DEC_BATCH * n_pages
    n_pool = n_used + max(1, n_used // 4)
    f32 = jnp.float32

    def nrm(k, shape, scale):
        return jax.random.normal(k, shape, f32) * scale

    def gain(k, shape):
        return 1.0 + 0.01 * jax.random.normal(k, shape, f32)

    page_table = jax.random.permutation(ks[6], n_pool)[:n_used].reshape(DEC_BATCH, n_pages).astype(jnp.int32)
    return {
        "x_prompt": nrm(ks[0], (BATCH, SEQ, D_MODEL), 1.0),
        "x_sample": nrm(ks[1], (DEC_BATCH, DEC_SEQ, D_MODEL), 1.0),
        "cache_k": nrm(ks[2], (DEPTH, n_pool, PAGE_SIZE, N_HEADS, HEAD_DIM), 1.0),
        "cache_v": nrm(ks[3], (DEPTH, n_pool, PAGE_SIZE, N_HEADS, HEAD_DIM), 1.0),
        "cache_idx_k": nrm(ks[4], (DEPTH, n_pool, PAGE_SIZE, IDX_DIM), 1.0),
        "state_conv": nrm(ks[5], (DEPTH, DEC_BATCH, CONV_W - 1, CONV_DIM), 1.0),
        "page_table": page_table,
        "rel_bias": nrm(ks[7], (N_BUCKETS, N_HEADS), 0.5),
        "norm_ffn1": gain(ks[8], (DEPTH, D_MODEL)),
        "w_ffn1_in": nrm(ks[9], (DEPTH, D_MODEL, 2 * D_FF), D_MODEL ** -0.5),
        "w_ffn1_out": nrm(ks[10], (DEPTH, D_FF, D_MODEL), D_FF ** -0.5),
        "norm_mix": gain(ks[11], (DEPTH, D_MODEL)),
        "w_in": nrm(ks[12], (DEPTH, D_MODEL, N_IN), D_MODEL ** -0.5),
        "conv_w": nrm(ks[13], (DEPTH, CONV_W, CONV_DIM), CONV_W ** -0.5),
        "w_o_conv": nrm(ks[14], (DEPTH, CONV_DIM, D_MODEL), CONV_DIM ** -0.5),
        "w_o_attn": nrm(ks[15], (DEPTH, ATT_DIM, D_MODEL), ATT_DIM ** -0.5),
        "w_out": nrm(ks[16], (DEPTH, D_MODEL, D_MODEL), D_MODEL ** -0.5),
        "norm_ffn2": gain(ks[17], (DEPTH, D_MODEL)),
        "w_ffn2_in": nrm(ks[18], (DEPTH, D_MODEL, 2 * D_FF), D_MODEL ** -0.5),
        "w_ffn2_out": nrm(ks[19], (DEPTH, D_FF, D_MODEL), D_FF ** -0.5),
        "norm_final": gain(ks[20], (D_MODEL,)),
    }


def reference(x_prompt, x_sample, cache_k, cache_v, cache_idx_k, state_conv, page_table, rel_bias,
              norm_ffn1, w_ffn1_in, w_ffn1_out, norm_mix, w_in, conv_w, w_o_conv, w_o_attn, w_out,
              norm_ffn2, w_ffn2_in, w_ffn2_out, norm_final):
    weights = (norm_ffn1, w_ffn1_in, w_ffn1_out, norm_mix, w_in, conv_w, w_o_conv, w_o_attn, w_out,
               norm_ffn2, w_ffn2_in, w_ffn2_out, norm_final)

    def attend_prompt(l, q, k, v, qi, wi, ki):
        return _prompt_attention(q, k, v, qi, wi, ki, rel_bias)

    def attend_sample(l, q, k, v, qi, wi, ki):
        return _sample_attention(l, q, k, v, qi, wi, ki, cache_k, cache_v, cache_idx_k, page_table, rel_bias)

    conv_zero = jnp.zeros((DEPTH, x_prompt.shape[0], CONV_W - 1, CONV_DIM), x_prompt.dtype)
    y_prompt, k_p, v_p, ik_p, conv_p = _trunk(x_prompt, conv_zero, attend_prompt, *weights)
    y_sample, k_s, v_s, ik_s, conv_s = _trunk(x_sample, state_conv, attend_sample, *weights)
    return (y_prompt, y_sample, k_p, v_p, ik_p, conv_p, k_s, v_s, ik_s, conv_s)
```

```python
import functools
import math

import jax
import jax.numpy as jnp
from jax import lax
from jax.experimental import pallas as pl
from jax.experimental.pallas import tpu as pltpu

F32 = jnp.float32
BF16 = jnp.bfloat16
I32 = jnp.int32

EPS = 1e-6
NEG_INF = -1e30
HEAD_DIM = 64
N_HEADS = 8
IDX_HEADS = 8
IDX_DIM = 64
TOPK_MAX = 256
N_BUCKETS = 32
REL_MAX_DIST = 128
CONV_W = 3
LANES = 128
HEADS_PER_VREG = LANES // HEAD_DIM
VMEM_LIMIT_BYTES = 56 * 1024 * 1024
ROW_TILE = 512
ATT_TILE = 256
FFN_CHUNK = 256
INT_MAX = 2**31 - 1


def _cparams(n_axes):
    return pltpu.CompilerParams(dimension_semantics=("arbitrary",) * n_axes,
                                vmem_limit_bytes=VMEM_LIMIT_BYTES)


def _resident(shape, index_map):
    return pl.BlockSpec(shape, index_map, pipeline_mode=pl.Buffered(1))


def _rms(x, g):
    return x * lax.rsqrt(jnp.mean(x * x, axis=-1, keepdims=True) + EPS) * g


def _sigmoid(x):
    return 1.0 / (1.0 + jnp.exp(-x))


def _dot(a, b):
    return jnp.dot(a, b, preferred_element_type=F32)


def _dot_nt(a, b):
    return lax.dot_general(a, b, (((1,), (1,)), ((), ())), preferred_element_type=F32)


def _ffn_kernel(x_ref, g_ref, wgu_ref, wd_ref, gf_ref, o_ref, acc_ref, *, d_ff, final_norm):
    x = x_ref[...]
    xb = _rms(x, g_ref[...]).astype(BF16)
    for j in range(d_ff // FFN_CHUNK):
        lo = j * FFN_CHUNK
        a = _dot(xb, wgu_ref[:, lo:lo + FFN_CHUNK])
        b = _dot(xb, wgu_ref[:, d_ff + lo:d_ff + lo + FFN_CHUNK])
        hid = (a * _sigmoid(a) * b).astype(BF16)
        part = _dot(hid, wd_ref[lo:lo + FFN_CHUNK, :])
        if j == 0:
            acc_ref[...] = part
        else:
            acc_ref[...] += part
    y = x + 0.5 * acc_ref[...]
    if final_norm:
        y = _rms(y, gf_ref[...])
    o_ref[...] = y


def _ffn(x, g, wgu, wd, g_final, final_norm):
    m, d = x.shape
    d_ff = wd.shape[0]
    tm = min(ROW_TILE, m)
    return pl.pallas_call(
        functools.partial(_ffn_kernel, d_ff=d_ff, final_norm=final_norm),
        grid=(m // tm,),
        in_specs=[pl.BlockSpec((tm, d), lambda i: (i, 0)),
                  _resident((1, d), lambda i: (0, 0)),
                  _resident((d, 2 * d_ff), lambda i: (0, 0)),
                  _resident((d_ff, d), lambda i: (0, 0)),
                  _resident((1, d), lambda i: (0, 0))],
        out_specs=pl.BlockSpec((tm, d), lambda i: (i, 0)),
        out_shape=jax.ShapeDtypeStruct((m, d), F32),
        scratch_shapes=[pltpu.VMEM((tm, d), F32)],
        compiler_params=_cparams(1),
        name="ffn",
    )(x, g, wgu, wd, g_final)


def _proj_kernel(h_ref, g_ref, w_ref, z_ref, bc_ref, sgc_ref, sga_ref, q_ref, qi_ref, wi_ref,
                 k_ref, v_ref, ki_ref, vb_ref, *t_refs, c, a, d, att_tile, emit_t):
    u = _rms(h_ref[...], g_ref[...]).astype(BF16)

    def mm(lo, hi):
        return _dot(u, w_ref[:, lo:hi])

    z_ref[...] = mm(2 * c, 3 * c) * mm(0, c)
    bc_ref[...] = mm(c, 2 * c)
    o = 3 * c
    q_ref[...] = (mm(o, o + a) * HEAD_DIM ** -0.5).astype(BF16)
    k = mm(o + a, o + 2 * a)
    k_ref[...] = k
    v = mm(o + 2 * a, o + 3 * a)
    v_ref[...] = v
    vb_ref[...] = v.astype(BF16)
    o += 3 * a
    qi_ref[...] = (mm(o, o + IDX_HEADS * IDX_DIM) * IDX_DIM ** -0.5).astype(BF16)
    o += IDX_HEADS * IDX_DIM
    kw = mm(o, o + LANES)
    ki_ref[...] = kw[:, :IDX_DIM]
    wi_ref[...] = mm(o + LANES, o + 2 * LANES)[:, :IDX_HEADS] * IDX_HEADS ** -0.5
    o += 2 * LANES
    sgc_ref[...] = _sigmoid(mm(o, o + d))
    sga_ref[...] = _sigmoid(mm(o + d, o + 2 * d))
    if emit_t:
        kt_ref, kit_ref = t_refs
        for j in range(k.shape[0] // att_tile):
            rows = slice(j * att_tile, (j + 1) * att_tile)
            kt_ref[0, j] = k[rows, :].T.astype(BF16)
            kwt = kw[rows, :].T[:IDX_DIM, :].astype(BF16)
            kit_ref[0, j] = jnp.concatenate([kwt] * HEADS_PER_VREG, axis=0)


def _proj(h, g, w, batch, att_tile, emit_t):
    m, d = h.shape
    n_cols = w.shape[1]
    c = a = d // 2
    seq = m // batch
    tm = min(ROW_TILE, seq)
    tiles_per_seq = seq // tm
    row = lambda i: (i, 0)
    out_shapes = [jax.ShapeDtypeStruct((m, c), F32),
                  jax.ShapeDtypeStruct((m, c), F32),
                  jax.ShapeDtypeStruct((m, d), F32),
                  jax.ShapeDtypeStruct((m, d), F32),
                  jax.ShapeDtypeStruct((m, a), BF16),
                  jax.ShapeDtypeStruct((m, a), BF16),
                  jax.ShapeDtypeStruct((m, IDX_HEADS), F32),
                  jax.ShapeDtypeStruct((m, a), F32),
                  jax.ShapeDtypeStruct((m, a), F32),
                  jax.ShapeDtypeStruct((m, IDX_DIM), F32),
                  jax.ShapeDtypeStruct((m, a), BF16)]
    out_specs = [pl.BlockSpec((tm, s.shape[1]), row) for s in out_shapes]
    if emit_t:
        n_chunks = seq // att_tile
        per_tile = tm // att_tile
        tmap = lambda i: (i // tiles_per_seq, i % tiles_per_seq, 0, 0)
        out_shapes += [jax.ShapeDtypeStruct((batch, n_chunks, a, att_tile), BF16),
                       jax.ShapeDtypeStruct((batch, n_chunks, LANES, att_tile), BF16)]
        out_specs += [pl.BlockSpec((1, per_tile, a, att_tile), tmap),
                      pl.BlockSpec((1, per_tile, LANES, att_tile), tmap)]
    return pl.pallas_call(
        functools.partial(_proj_kernel, c=c, a=a, d=d, att_tile=att_tile, emit_t=emit_t),
        grid=(m // tm,),
        in_specs=[pl.BlockSpec((tm, d), row),
                  _resident((1, d), lambda i: (0, 0)),
                  _resident((d, n_cols), lambda i: (0, 0))],
        out_specs=out_specs,
        out_shape=out_shapes,
        compiler_params=_cparams(1),
        name="proj",
    )(h, g, w)


def _sortable(x):
    x = jnp.where(x == 0.0, 0.0, x)
    b = lax.bitcast_convert_type(x, I32)
    return b ^ ((b >> 31) & I32(INT_MAX))


def _fold_lanes(x):
    out = x[:, :LANES]
    for j in range(1, x.shape[1] // LANES):
        out = out + x[:, j * LANES:(j + 1) * LANES]
    return out


def _count(keys_ref, n_chunks, pred):
    rows = keys_ref.shape[1]

    def body(ci, acc):
        return acc + _fold_lanes(jnp.where(pred(keys_ref[ci], ci), 1.0, 0.0))

    acc = lax.fori_loop(0, n_chunks, body, jnp.zeros((rows, LANES), F32))
    return jnp.sum(acc, axis=-1, keepdims=True)


def _select_threshold(keys_ref, n_chunks, top_k, tie_ref):
    _, rows, cols = keys_ref.shape
    sign = I32(-2**31)
    kf = float(top_k)

    def bit_step(it, prefix):
        cand = prefix | lax.shift_left(I32(1), 31 - it)
        cnt = _count(keys_ref, n_chunks, lambda kc, ci: kc >= (cand ^ sign))
        return jnp.where(cnt >= kf, cand, prefix)

    thr = lax.fori_loop(0, 32, bit_step, jnp.zeros((rows, 1), I32)) ^ sign
    cnt_gt = _count(keys_ref, n_chunks, lambda kc, ci: kc > thr)
    cnt_ge = _count(keys_ref, n_chunks, lambda kc, ci: kc >= thr)
    tie_ref[...] = jnp.full((rows, 1), INT_MAX, I32)

    @pl.when(jnp.max(cnt_ge) > kf)
    def _():
        need = kf - cnt_gt
        col = lax.broadcasted_iota(I32, (rows, cols), 1)
        n_bits = max(1, (keys_ref.shape[0] * cols - 1).bit_length())

        def idx_step(it, j):
            cand = j | lax.shift_left(I32(1), n_bits - 1 - it)
            cnt = _count(keys_ref, n_chunks,
                         lambda kc, ci: jnp.where(kc == thr, ci * cols + col, INT_MAX) < cand)
            return jnp.where(cnt < need, cand, j)

        tie_ref[...] = lax.fori_loop(0, n_bits, idx_step, jnp.zeros((rows, 1), I32))

    return thr


def _selected(kc, col_global, thr, tie):
    return jnp.logical_or(kc > thr, jnp.logical_and(kc == thr, col_global <= tie))


def _attn_kernel(far_ref, q_ref, qi_ref, wi_ref, kt_ref, v_ref, kit_ref, bias0_ref, bias1_ref, o_ref,
                 keys_ref, qm_ref, qim_ref, m_ref, l_ref, acc_ref, tie_ref, *, top_k):
    t = q_ref.shape[0]
    i = pl.program_id(1)
    lane_head = lax.shift_right_logical(lax.broadcasted_iota(I32, (t, LANES), 1), HEAD_DIM.bit_length() - 1)
    row = lax.broadcasted_iota(I32, (t, t), 0)
    col = lax.broadcasted_iota(I32, (t, t), 1)

    for h in range(N_HEADS):
        pair = slice((h // HEADS_PER_VREG) * LANES, (h // HEADS_PER_VREG + 1) * LANES)
        keep = lane_head == (h % HEADS_PER_VREG)
        qm_ref[h] = jnp.where(keep, q_ref[:, pair], 0.0).astype(BF16)
        qim_ref[h] = jnp.where(keep, qi_ref[:, pair], 0.0).astype(BF16)
    w = wi_ref[...]

    def score_chunk(ci, diagonal):
        kic = kit_ref[0, ci]
        sc = jnp.zeros((t, t), F32)
        for h in range(IDX_HEADS):
            sc = sc + w[:, h:h + 1] * jnp.maximum(_dot(qim_ref[h], kic), 0.0)
        if diagonal:
            sc = jnp.where(col <= row, sc, NEG_INF)
        keys_ref[ci] = _sortable(sc)

    def score_body(ci, carry):
        score_chunk(ci, False)
        return carry

    lax.fori_loop(0, i, score_body, 0)
    score_chunk(i, True)

    thr = _select_threshold(keys_ref, i + 1, top_k, tie_ref)
    tie = tie_ref[...]

    m_ref[...] = jnp.full(m_ref.shape, NEG_INF, F32)
    l_ref[...] = jnp.zeros(l_ref.shape, F32)
    acc_ref[...] = jnp.zeros(acc_ref.shape, F32)

    def attend_chunk(ci, mode):
        sel = _selected(keys_ref[ci], ci * t + col, thr, tie)
        if mode == "diag":
            sel = jnp.logical_and(sel, col <= row)
        mask_add = jnp.where(sel, 0.0, NEG_INF)
        ktc = kt_ref[0, ci]
        vc = v_ref[0, ci]
        for h in range(N_HEADS):
            pair = slice((h // HEADS_PER_VREG) * LANES, (h // HEADS_PER_VREG + 1) * LANES)
            s = _dot(qm_ref[h], ktc[pair, :])
            if mode == "far":
                s = s + far_ref[h] + mask_add
            elif mode == "prev":
                s = s + bias1_ref[h] + mask_add
            else:
                s = s + bias0_ref[h] + mask_add
            m_old = m_ref[h]
            m_new = jnp.maximum(m_old, jnp.max(s, axis=-1, keepdims=True))
            alpha = jnp.exp(m_old - m_new)
            p = jnp.exp(s - m_new)
            l_ref[h] = alpha * l_ref[h] + jnp.sum(p, axis=-1, keepdims=True)
            acc_ref[h] = alpha * acc_ref[h] + _dot(p.astype(BF16), vc[:, pair])
            m_ref[h] = m_new

    def far_body(ci, carry):
        attend_chunk(ci, "far")
        return carry

    lax.fori_loop(0, i - 1, far_body, 0)

    @pl.when(i >= 1)
    def _():
        attend_chunk(i - 1, "prev")

    attend_chunk(i, "diag")

    for pr in range(N_HEADS // HEADS_PER_VREG):
        h0 = pr * HEADS_PER_VREG
        out = acc_ref[h0] / l_ref[h0]
        for hh in range(1, HEADS_PER_VREG):
            out = jnp.where(lane_head == hh, acc_ref[h0 + hh] / l_ref[h0 + hh], out)
        o_ref[:, pr * LANES:(pr + 1) * LANES] = out.astype(BF16)


def _prompt_attention(q, qi, wi, kt, vb, kit, bias0, bias1, far_bias, top_k):
    batch, n_chunks, a, t = kt.shape
    m = q.shape[0]
    qmap = lambda b, i: (b * n_chunks + i, 0)
    bmap = lambda b, i: (b, 0, 0, 0)
    cmap = lambda b, i: (0, 0, 0)
    return pl.pallas_call(
        functools.partial(_attn_kernel, top_k=top_k),
        grid=(batch, n_chunks),
        in_specs=[pl.BlockSpec(memory_space=pltpu.SMEM),
                  pl.BlockSpec((t, a), qmap),
                  pl.BlockSpec((t, a), qmap),
                  pl.BlockSpec((t, IDX_HEADS), qmap),
                  _resident((1, n_chunks, a, t), bmap),
                  _resident((1, n_chunks, t, a), bmap),
                  _resident((1, n_chunks, LANES, t), bmap),
                  _resident((N_HEADS, t, t), cmap),
                  _resident((N_HEADS, t, t), cmap)],
        out_specs=pl.BlockSpec((t, a), qmap),
        out_shape=jax.ShapeDtypeStruct((m, a), BF16),
        scratch_shapes=[pltpu.VMEM((n_chunks, t, t), I32),
                        pltpu.VMEM((N_HEADS, t, LANES), BF16),
                        pltpu.VMEM((IDX_HEADS, t, LANES), BF16),
                        pltpu.VMEM((N_HEADS, t, 1), F32),
                        pltpu.VMEM((N_HEADS, t, 1), F32),
                        pltpu.VMEM((N_HEADS, t, LANES), F32),
                        pltpu.VMEM((t, 1), I32)],
        compiler_params=_cparams(2),
        name="prompt_attn",
    )(far_bias, q, qi, wi, kt, vb.reshape(batch, n_chunks, t, a), kit, bias0, bias1)


def _bias_of_dist(dist, rb_ref, h):
    n = jnp.maximum(dist, 0)
    max_exact = N_BUCKETS // 2
    nf = jnp.maximum(n, 1).astype(F32)
    large = max_exact + (jnp.log(nf / max_exact) / math.log(REL_MAX_DIST / max_exact)
                         * (N_BUCKETS - max_exact)).astype(I32)
    large = jnp.minimum(large, N_BUCKETS - 1)
    bucket = jnp.where(n < max_exact, n, large)
    out = jnp.zeros(dist.shape, F32)
    for nb in range(N_BUCKETS):
        out = jnp.where(bucket == nb, rb_ref[nb, h], out)
    return out


def _bias_kernel(rb_ref, b0_ref, b1_ref, bs_ref, *, t, past, page):
    row = lax.broadcasted_iota(I32, (t, t), 0)
    col = lax.broadcasted_iota(I32, (t, t), 1)
    key_lane = lax.shift_right_logical(lax.broadcasted_iota(I32, (1, page * N_HEADS), 1),
                                       N_HEADS.bit_length() - 1)
    for h in range(N_HEADS):
        b0_ref[h] = _bias_of_dist(row - col, rb_ref, h)
        b1_ref[h] = _bias_of_dist(t + row - col, rb_ref, h)
        for p in range(bs_ref.shape[0]):
            bs_ref[p, h:h + 1, :] = _bias_of_dist(past - (p * page + key_lane), rb_ref, h)


def _bias_tables(rel_bias, t, past, page):
    n_chunks_s = past // page + 1
    return pl.pallas_call(
        functools.partial(_bias_kernel, t=t, past=past, page=page),
        in_specs=[pl.BlockSpec(memory_space=pltpu.SMEM)],
        out_shape=[jax.ShapeDtypeStruct((N_HEADS, t, t), F32),
                   jax.ShapeDtypeStruct((N_HEADS, t, t), F32),
                   jax.ShapeDtypeStruct((n_chunks_s, N_HEADS, page * N_HEADS), F32)],
        compiler_params=pltpu.CompilerParams(vmem_limit_bytes=VMEM_LIMIT_BYTES),
        name="bias_tables",
    )(rel_bias)


def _sscore_kernel(pt_ref, qi_ref, wi_ref, ik_ref, o_ref):
    dots = _dot_nt(qi_ref[0], ik_ref[0, 0].astype(BF16))
    o_ref[0, 0] = jnp.sum(wi_ref[0] * jnp.maximum(dots, 0.0), axis=0, keepdims=True)


def _sample_scores(layer, page_table_flat, qi8, wi8, cache_idx_k, n_pages):
    db = qi8.shape[0]
    page = cache_idx_k.shape[2]
    return pl.pallas_call(
        _sscore_kernel,
        grid_spec=pltpu.PrefetchScalarGridSpec(
            num_scalar_prefetch=1,
            grid=(db, n_pages),
            in_specs=[pl.BlockSpec((1, IDX_HEADS, IDX_DIM), lambda b, p, pt: (b, 0, 0)),
                      pl.BlockSpec((1, IDX_HEADS, 1), lambda b, p, pt: (b, 0, 0)),
                      pl.BlockSpec((1, 1, page, IDX_DIM),
                                   lambda b, p, pt: (layer, pt[b * n_pages + p], 0, 0))],
            out_specs=pl.BlockSpec((1, 1, 1, page), lambda b, p, pt: (p, b, 0, 0))),
        out_shape=jax.ShapeDtypeStruct((n_pages, db, 1, page), F32),
        compiler_params=_cparams(2),
        name="sample_scores",
    )(page_table_flat, qi8, wi8, cache_idx_k)


def _sselect_kernel(sc_ref, qi_ref, kin_ref, wi_ref, o_ref, keys_ref, tie_ref, *, top_k, past):
    n_pages, db, page = sc_ref.shape
    for p in range(n_pages):
        keys_ref[p] = _sortable(sc_ref[p])
    kin = kin_ref[...].astype(BF16).astype(F32)
    qif = qi_ref[...].astype(F32)
    w = wi_ref[...]
    s_new = jnp.zeros((db, 1), F32)
    for h in range(IDX_HEADS):
        qh = qif[:, h * IDX_DIM:(h + 1) * IDX_DIM]
        s_new = s_new + w[:, h:h + 1] * jnp.maximum(jnp.sum(qh * kin, axis=-1, keepdims=True), 0.0)
    col = lax.broadcasted_iota(I32, (db, page), 1)
    keys_ref[n_pages] = _sortable(jnp.where(col == 0, s_new, NEG_INF))

    thr = _select_threshold(keys_ref, n_pages + 1, top_k, tie_ref)
    tie = tie_ref[...]
    src = lax.broadcasted_iota(I32, (page, page * N_HEADS), 0)
    dst = lax.shift_right_logical(lax.broadcasted_iota(I32, (page, page * N_HEADS), 1),
                                  N_HEADS.bit_length() - 1)
    expand = jnp.where(src == dst, 1.0, 0.0).astype(BF16)
    for p in range(n_pages + 1):
        colg = p * page + col
        sel = jnp.logical_and(colg <= past, _selected(keys_ref[p], colg, thr, tie))
        o_ref[p] = _dot(jnp.where(sel, 1.0, 0.0).astype(BF16), expand)


def _sample_select(scores, qi, ki_new, wi, top_k, past):
    n_pages, db, page = scores.shape
    return pl.pallas_call(
        functools.partial(_sselect_kernel, top_k=top_k, past=past),
        out_shape=jax.ShapeDtypeStruct((n_pages + 1, db, page * N_HEADS), F32),
        scratch_shapes=[pltpu.VMEM((n_pages + 1, db, page), I32),
                        pltpu.VMEM((db, 1), I32)],
        compiler_params=pltpu.CompilerParams(vmem_limit_bytes=VMEM_LIMIT_BYTES),
        name="sample_select",
    )(scores, qi, ki_new, wi)


def _sattn_kernel(pt_ref, q_ref, sel_ref, bias_ref, kc_ref, vc_ref, kn_ref, vn_ref, o_ref,
                  m_ref, l_ref, acc_ref, *, n_pages):
    p = pl.program_id(1)

    @pl.when(p == 0)
    def _():
        m_ref[...] = jnp.full(m_ref.shape, NEG_INF, F32)
        l_ref[...] = jnp.zeros(l_ref.shape, F32)
        acc_ref[...] = jnp.zeros(acc_ref.shape, F32)

    page = kc_ref.shape[2]
    rows = page * N_HEADS
    is_new = p == n_pages
    pad = jnp.zeros((rows - N_HEADS, HEAD_DIM), F32)
    kp = jnp.where(is_new, jnp.concatenate([kn_ref[0], pad], axis=0), kc_ref[0, 0].reshape(rows, HEAD_DIM))
    vp = jnp.where(is_new, jnp.concatenate([vn_ref[0], pad], axis=0), vc_ref[0, 0].reshape(rows, HEAD_DIM))
    g = _dot_nt(q_ref[0], kp.astype(BF16))
    lane_head = lax.broadcasted_iota(I32, (N_HEADS, rows), 1) & (N_HEADS - 1)
    own = lane_head == lax.broadcasted_iota(I32, (N_HEADS, rows), 0)
    keep = jnp.logical_and(own, sel_ref[0, 0] > 0.5)
    s = jnp.where(keep, g + bias_ref[0], NEG_INF)
    m_old = m_ref[...]
    m_new = jnp.maximum(m_old, jnp.max(s, axis=-1, keepdims=True))
    alpha = jnp.exp(m_old - m_new)
    pe = jnp.exp(s - m_new)
    l_ref[...] = alpha * l_ref[...] + jnp.sum(pe, axis=-1, keepdims=True)
    acc_ref[...] = alpha * acc_ref[...] + _dot(pe.astype(BF16), vp.astype(BF16))
    m_ref[...] = m_new

    @pl.when(is_new)
    def _():
        o_ref[0] = (acc_ref[...] / l_ref[...]).astype(BF16)


def _sample_attention(layer, page_table_flat, q8, sel, bias_s, cache_k, cache_v, k_new8, v_new8, n_pages):
    db = q8.shape[0]
    page = cache_k.shape[2]
    rows = page * N_HEADS
    last = n_pages - 1
    cache_map = lambda b, p, pt: (layer, pt[b * n_pages + jnp.minimum(p, last)], 0, 0, 0)
    per_sample = lambda b, p, pt: (b, 0, 0)
    return pl.pallas_call(
        functools.partial(_sattn_kernel, n_pages=n_pages),
        grid_spec=pltpu.PrefetchScalarGridSpec(
            num_scalar_prefetch=1,
            grid=(db, n_pages + 1),
            in_specs=[pl.BlockSpec((1, N_HEADS, HEAD_DIM), per_sample),
                      pl.BlockSpec((1, 1, 1, rows), lambda b, p, pt: (p, b, 0, 0)),
                      pl.BlockSpec((1, N_HEADS, rows), lambda b, p, pt: (p, 0, 0)),
                      pl.BlockSpec((1, 1, page, N_HEADS, HEAD_DIM), cache_map),
                      pl.BlockSpec((1, 1, page, N_HEADS, HEAD_DIM), cache_map),
                      pl.BlockSpec((1, N_HEADS, HEAD_DIM), per_sample),
                      pl.BlockSpec((1, N_HEADS, HEAD_DIM), per_sample)],
            out_specs=pl.BlockSpec((1, N_HEADS, HEAD_DIM), per_sample),
            scratch_shapes=[pltpu.VMEM((N_HEADS, 1), F32),
                            pltpu.VMEM((N_HEADS, 1), F32),
                            pltpu.VMEM((N_HEADS, HEAD_DIM), F32)]),
        out_shape=jax.ShapeDtypeStruct((db, N_HEADS, HEAD_DIM), BF16),
        compiler_params=_cparams(2),
        name="sample_attn",
    )(page_table_flat, q8, sel, bias_s, cache_k, cache_v, k_new8, v_new8)


def _merge_kernel(h_ref, z_ref, zp1_ref, zp2_ref, prev_ref, bc_ref, sgc_ref, sga_ref, o_ref, cw_ref,
                  woc_ref, woa_ref, wout_ref, x_ref, *, tiles_per_seq, sequence):
    z = z_ref[...]
    if sequence:
        first = (pl.program_id(0) % tiles_per_seq) == 0
        before1 = jnp.where(first, prev_ref[0, 1:2, :], zp1_ref[7:8, :])
        before2 = jnp.where(first, prev_ref[0, 0:1, :], zp1_ref[6:7, :])
        rowi = lax.broadcasted_iota(I32, z.shape, 0)
        z1 = jnp.where(rowi == 0, before1, pltpu.roll(z, 1, 0))
        z2 = jnp.where(rowi == 0, before2, jnp.where(rowi == 1, before1, pltpu.roll(z, 2, 0)))
    else:
        z1 = zp1_ref[...]
        z2 = zp2_ref[...]
    conv = cw_ref[0:1, :] * z2 + cw_ref[1:2, :] * z1 + cw_ref[2:3, :] * z
    y_conv = _dot((bc_ref[...] * conv).astype(BF16), woc_ref[...])
    y_attn = _dot(o_ref[...], woa_ref[...])
    merged = sgc_ref[...] * y_conv + sga_ref[...] * y_attn
    x_ref[...] = h_ref[...] + _dot(merged.astype(BF16), wout_ref[...])


def _merge(h, z, zp1, zp2, prev, bc, sgc, sga, o, cw, woc, woa, wout, batch, sequence):
    m, d = h.shape
    c = z.shape[1]
    seq = m // batch
    tm = min(ROW_TILE, seq) if sequence else min(ROW_TILE, m)
    tiles_per_seq = seq // tm if sequence else 1
    row = lambda i: (i, 0)
    const = lambda i: (0, 0)
    if sequence:
        halo = tm // 8
        zp1_spec = pl.BlockSpec((8, c), lambda i: (jnp.maximum(i * halo - 1, 0), 0))
        zp2_spec = pl.BlockSpec((8, c), lambda i: (0, 0))
        prev_spec = pl.BlockSpec((1, CONV_W - 1, c), lambda i: (i // tiles_per_seq, 0, 0))
    else:
        zp1_spec = pl.BlockSpec((tm, c), row)
        zp2_spec = pl.BlockSpec((tm, c), row)
        prev_spec = pl.BlockSpec((1, CONV_W - 1, c), lambda i: (0, 0, 0))
    return pl.pallas_call(
        functools.partial(_merge_kernel, tiles_per_seq=tiles_per_seq, sequence=sequence),
        grid=(m // tm,),
        in_specs=[pl.BlockSpec((tm, d), row),
                  pl.BlockSpec((tm, c), row),
                  zp1_spec, zp2_spec, prev_spec,
                  pl.BlockSpec((tm, c), row),
                  pl.BlockSpec((tm, d), row),
                  pl.BlockSpec((tm, d), row),
                  pl.BlockSpec((tm, o.shape[1]), row),
                  _resident((CONV_W, c), const),
                  _resident(woc.shape, const),
                  _resident(woa.shape, const),
                  _resident(wout.shape, const)],
        out_specs=pl.BlockSpec((tm, d), row),
        out_shape=jax.ShapeDtypeStruct((m, d), F32),
        compiler_params=_cparams(1),
        name="merge",
    )(h, z, zp1, zp2, prev, bc, sgc, sga, o, cw, woc, woa, wout)


def _arrange_w_in(w, d):
    c = a = d // 2
    o = 3 * c + 3 * a + IDX_HEADS * IDX_DIM
    head = w[:, :o]
    ki = jnp.pad(w[:, o:o + IDX_DIM], ((0, 0), (0, LANES - IDX_DIM)))
    wi = jnp.pad(w[:, o + IDX_DIM:o + IDX_DIM + IDX_HEADS], ((0, 0), (0, LANES - IDX_HEADS)))
    gates = w[:, o + IDX_DIM + IDX_HEADS:]
    return jnp.concatenate([head, ki, wi, gates], axis=1).astype(BF16)


def kernel(x_prompt, x_sample, cache_k, cache_v, cache_idx_k, state_conv, page_table, rel_bias,
           norm_ffn1, w_ffn1_in, w_ffn1_out, norm_mix, w_in, conv_w, w_o_conv, w_o_attn, w_out,
           norm_ffn2, w_ffn2_in, w_ffn2_out, norm_final):
    batch, seq, d = x_prompt.shape
    db, dec_seq, _ = x_sample.shape
    assert dec_seq == 1
    depth = w_in.shape[0]
    c = a = d // 2
    page = cache_k.shape[2]
    n_pages = page_table.shape[1]
    past = n_pages * page
    t = min(ATT_TILE, seq)
    assert seq % t == 0 and seq % min(ROW_TILE, seq) == 0 and min(ROW_TILE, seq) % t == 0
    top_k_p = min(TOPK_MAX, seq // 4)
    top_k_s = min(TOPK_MAX, (past + dec_seq) // 4)

    bias0, bias1, bias_s = _bias_tables(rel_bias, t, past, page)
    far_bias = rel_bias[N_BUCKETS - 1]
    pt_flat = page_table.reshape(-1)
    g_final = norm_final.reshape(1, d)
    conv_zero = jnp.zeros((batch, CONV_W - 1, c), F32)

    xp = x_prompt.reshape(batch * seq, d)
    xs = x_sample.reshape(db, d)
    outs_p = {"k": [], "v": [], "ik": [], "conv": []}
    outs_s = {"k": [], "v": [], "ik": [], "conv": []}

    for l in range(depth):
        last = l == depth - 1
        g1 = norm_ffn1[l].reshape(1, d)
        g2 = norm_ffn2[l].reshape(1, d)
        gm = norm_mix[l].reshape(1, d)
        w1_in, w1_out = w_ffn1_in[l].astype(BF16), w_ffn1_out[l].astype(BF16)
        w2_in, w2_out = w_ffn2_in[l].astype(BF16), w_ffn2_out[l].astype(BF16)
        wp = _arrange_w_in(w_in[l], d)
        woc, woa, wout = w_o_conv[l].astype(BF16), w_o_attn[l].astype(BF16), w_out[l].astype(BF16)
        cw = conv_w[l]

        hp = _ffn(xp, g1, w1_in, w1_out, g_final, False)
        (z, bc, sgc, sga, q, qi, wi, k, v, ki, vb, kt, kit) = _proj(hp, gm, wp, batch, t, True)
        o = _prompt_attention(q, qi, wi, kt, vb, kit, bias0, bias1, far_bias, top_k_p)
        x2 = _merge(hp, z, z, z, conv_zero, bc, sgc, sga, o, cw, woc, woa, wout, batch, True)
        xp = _ffn(x2, g2, w2_in, w2_out, g_final, last)
        outs_p["k"].append(k)
        outs_p["v"].append(v)
        outs_p["ik"].append(ki)
        outs_p["conv"].append(z.reshape(batch, seq, c)[:, seq - (CONV_W - 1):])

        hs = _ffn(xs, g1, w1_in, w1_out, g_final, False)
        (z, bc, sgc, sga, q, qi, wi, k, v, ki, vb) = _proj(hs, gm, wp, 1, t, False)
        scores = _sample_scores(l, pt_flat, qi.reshape(db, IDX_HEADS, IDX_DIM),
                                wi.reshape(db, IDX_HEADS, 1), cache_idx_k, n_pages)
        sel = _sample_select(scores.reshape(n_pages, db, page), qi, ki, wi, top_k_s, past)
        o = _sample_attention(l, pt_flat, q.reshape(db, N_HEADS, HEAD_DIM),
                              sel.reshape(n_pages + 1, db, 1, page * N_HEADS), bias_s, cache_k, cache_v,
                              k.reshape(db, N_HEADS, HEAD_DIM), v.reshape(db, N_HEADS, HEAD_DIM), n_pages)
        prev = state_conv[l]
        x2 = _merge(hs, z, prev[:, 1], prev[:, 0], conv_zero, bc, sgc, sga, o.reshape(db, a), cw,
                    woc, woa, wout, 1, False)
        xs = _ffn(x2, g2, w2_in, w2_out, g_final, last)
        outs_s["k"].append(k)
        outs_s["v"].append(v)
        outs_s["ik"].append(ki)
        outs_s["conv"].append(jnp.stack([prev[:, 1], z], axis=1))

    def stacked(parts, shape):
        return jnp.stack(parts).reshape((depth,) + shape)

    return (xp.reshape(batch, seq, d),
            xs.reshape(db, dec_seq, d),
            stacked(outs_p["k"], (batch, seq, N_HEADS, HEAD_DIM)),
            stacked(outs_p["v"], (batch, seq, N_HEADS, HEAD_DIM)),
            stacked(outs_p["ik"], (batch, seq, IDX_DIM)),
            stacked(outs_p["conv"], (batch, CONV_W - 1, c)),
            stacked(outs_s["k"], (db, dec_seq, N_HEADS, HEAD_DIM)),
            stacked(outs_s["v"], (db, dec_seq, N_HEADS, HEAD_DIM)),
            stacked(outs_s["ik"], (db, dec_seq, IDX_DIM)),
            stacked(outs_s["conv"], (db, CONV_W - 1, c)))
```

```python
import functools
import math

import jax
import jax.numpy as jnp
from jax import lax
from jax.experimental import pallas as pl
from jax.experimental.pallas import tpu as pltpu

F32 = jnp.float32
BF16 = jnp.bfloat16
I32 = jnp.int32

EPS = 1e-6
NEG_INF = -1e30
HEAD_DIM = 64
N_HEADS = 8
IDX_HEADS = 8
IDX_DIM = 64
TOPK_MAX = 256
N_BUCKETS = 32
REL_MAX_DIST = 128
CONV_W = 3
LANES = 128
SUBLANES = 8
HEADS_PER_VREG = LANES // HEAD_DIM
VMEM_LIMIT_BYTES = 56 * 1024 * 1024
ROW_TILE = 512
ATT_TILE = 256
FFN_CHUNK = 256
INT_MAX = 2**31 - 1
LOG2E = math.log2(math.e)


def _cparams(n_axes):
    return pltpu.CompilerParams(dimension_semantics=("arbitrary",) * n_axes,
                                vmem_limit_bytes=VMEM_LIMIT_BYTES)


def _resident(shape, index_map):
    return pl.BlockSpec(shape, index_map, pipeline_mode=pl.Buffered(1))


def _rms(x, g):
    return x * lax.rsqrt(jnp.mean(x * x, axis=-1, keepdims=True) + EPS) * g


def _sigmoid(x):
    return 1.0 / (1.0 + jnp.exp(-x))


def _dot(a, b):
    return jnp.dot(a, b, preferred_element_type=F32)


def _dot_nt(a, b):
    return lax.dot_general(a, b, (((1,), (1,)), ((), ())), preferred_element_type=F32)


def _ffn_kernel(x_ref, g_ref, wgu_ref, wd_ref, gf_ref, o_ref, acc_ref, *, d_ff, final_norm):
    x = x_ref[...]
    xb = _rms(x, g_ref[...]).astype(BF16)
    for j in range(d_ff // FFN_CHUNK):
        lo = j * FFN_CHUNK
        a = _dot(xb, wgu_ref[:, lo:lo + FFN_CHUNK])
        b = _dot(xb, wgu_ref[:, d_ff + lo:d_ff + lo + FFN_CHUNK])
        hid = (a * _sigmoid(a) * b).astype(BF16)
        part = _dot(hid, wd_ref[lo:lo + FFN_CHUNK, :])
        if j == 0:
            acc_ref[...] = part
        else:
            acc_ref[...] += part
    y = x + 0.5 * acc_ref[...]
    if final_norm:
        y = _rms(y, gf_ref[...])
    o_ref[...] = y


def _ffn(x, g, wgu, wd, g_final, final_norm):
    m, d = x.shape
    d_ff = wd.shape[0]
    tm = min(ROW_TILE, m)
    return pl.pallas_call(
        functools.partial(_ffn_kernel, d_ff=d_ff, final_norm=final_norm),
        grid=(m // tm,),
        in_specs=[pl.BlockSpec((tm, d), lambda i: (i, 0)),
                  _resident((1, d), lambda i: (0, 0)),
                  _resident((d, 2 * d_ff), lambda i: (0, 0)),
                  _resident((d_ff, d), lambda i: (0, 0)),
                  _resident((1, d), lambda i: (0, 0))],
        out_specs=pl.BlockSpec((tm, d), lambda i: (i, 0)),
        out_shape=jax.ShapeDtypeStruct((m, d), F32),
        scratch_shapes=[pltpu.VMEM((tm, d), F32)],
        compiler_params=_cparams(1),
        name="ffn",
    )(x, g, wgu, wd, g_final)


def _proj_kernel(h_ref, g_ref, w_ref, z_ref, bc_ref, sgc_ref, sga_ref, kt_ref, vt_ref, kit_ref, *refs,
                 c, a, d, prompt):
    u = _rms(h_ref[...], g_ref[...]).astype(BF16)

    def mm(lo, hi):
        return _dot(u, w_ref[:, lo:hi])

    z_ref[...] = mm(2 * c, 3 * c) * mm(0, c)
    bc_ref[...] = mm(c, 2 * c)
    o = 3 * c
    q = mm(o, o + a)
    k = mm(o + a, o + 2 * a)
    v = mm(o + 2 * a, o + 3 * a)
    o += 3 * a
    qi = mm(o, o + IDX_HEADS * IDX_DIM) * IDX_DIM ** -0.5
    o += IDX_HEADS * IDX_DIM
    ki = mm(o, o + LANES)
    wi = mm(o + LANES, o + 2 * LANES) * IDX_HEADS ** -0.5
    o += 2 * LANES
    sgc_ref[...] = _sigmoid(mm(o, o + d))
    sga_ref[...] = _sigmoid(mm(o + d, o + 2 * d))
    kt = k.T
    vt = v.T
    kt_ref[0] = kt
    vt_ref[0] = vt
    kit_ref[0] = ki.T[:IDX_DIM, :]
    if prompt:
        kb_ref, ki2_ref, vtb_ref, qt_ref, qit_ref, wit_ref = refs
        kb_ref[...] = k.astype(BF16)
        ki2_ref[...] = (ki + pltpu.roll(ki, IDX_DIM, 1)).astype(BF16)
        t = vtb_ref.shape[3]
        for j in range(vtb_ref.shape[1]):
            vtb_ref[0, j] = vt[:, j * t:(j + 1) * t].astype(BF16)
        qt_ref[0] = (q * (HEAD_DIM ** -0.5 * LOG2E)).T.astype(BF16)
        qit_ref[0] = qi.T.astype(BF16)
        wit_ref[0] = wi.T[:IDX_HEADS, :]
    else:
        q_ref, qi_ref, wi_ref, k_ref, v_ref, ki_ref = refs
        q_ref[...] = (q * (HEAD_DIM ** -0.5 * LOG2E)).astype(BF16)
        qi_ref[...] = qi.astype(BF16)
        wi_ref[...] = wi[:, :IDX_HEADS]
        k_ref[...] = k
        v_ref[...] = v
        ki_ref[...] = ki[:, :IDX_DIM]


def _proj(h, g, w, batch, prompt, att_tile=None):
    m, d = h.shape
    n_cols = w.shape[1]
    c = a = d // 2
    seq = m // batch
    tm = min(ROW_TILE, seq)
    tiles_per_seq = seq // tm
    row = lambda i: (i, 0)
    fmaj = lambda i: (i // tiles_per_seq, 0, i % tiles_per_seq)
    chunked = lambda i: (i // tiles_per_seq, i % tiles_per_seq, 0, 0)

    def nat(cols, dtype):
        return jax.ShapeDtypeStruct((m, cols), dtype), pl.BlockSpec((tm, cols), row)

    def fm(rows, dtype):
        return jax.ShapeDtypeStruct((batch, rows, seq), dtype), pl.BlockSpec((1, rows, tm), fmaj)

    outs = [nat(c, F32),
            nat(c, F32),
            nat(d, F32),
            nat(d, F32),
            fm(a, F32),
            fm(a, F32),
            fm(IDX_DIM, F32)]
    if prompt:
        assert tm % att_tile == 0
        outs += [nat(a, BF16),
                 nat(LANES, BF16),
                 (jax.ShapeDtypeStruct((batch, seq // att_tile, a, att_tile), BF16),
                  pl.BlockSpec((1, tm // att_tile, a, att_tile), chunked)),
                 fm(a, BF16),
                 fm(a, BF16),
                 fm(IDX_HEADS, F32)]
    else:
        outs += [nat(a, BF16), nat(a, BF16), nat(IDX_HEADS, F32), nat(a, F32), nat(a, F32), nat(IDX_DIM, F32)]
    return pl.pallas_call(
        functools.partial(_proj_kernel, c=c, a=a, d=d, prompt=prompt),
        grid=(m // tm,),
        in_specs=[pl.BlockSpec((tm, d), row),
                  _resident((1, d), lambda i: (0, 0)),
                  _resident((d, n_cols), lambda i: (0, 0))],
        out_specs=[o[1] for o in outs],
        out_shape=[o[0] for o in outs],
        compiler_params=_cparams(1),
        name="proj",
    )(h, g, w)


def _sortable(x):
    x = jnp.where(x == 0.0, 0.0, x)
    b = lax.bitcast_convert_type(x, I32)
    return b ^ ((b >> 31) & I32(INT_MAX))


def _count(keys_ref, n_chunks, pred):
    kc, nq = keys_ref.shape[1:]

    def body(ci, acc):
        hit = jnp.where(pred(keys_ref[ci], ci), 1.0, 0.0)
        return acc + jnp.sum(hit.reshape(kc // SUBLANES, SUBLANES, nq), axis=0)

    acc = lax.fori_loop(0, n_chunks, body, jnp.zeros((SUBLANES, nq), F32))
    return jnp.sum(acc, axis=0, keepdims=True)


def _select_threshold(keys_ref, n_chunks, top_k, tie_ref):
    _, kc, nq = keys_ref.shape
    sign = I32(-2**31)
    kf = float(top_k)

    def bit_step(it, prefix):
        cand = prefix | lax.shift_left(I32(1), 31 - it)
        cnt = _count(keys_ref, n_chunks, lambda kk, ci: kk >= (cand ^ sign))
        return jnp.where(cnt >= kf, cand, prefix)

    thr = lax.fori_loop(0, 32, bit_step, jnp.zeros((1, nq), I32)) ^ sign
    cnt_gt = _count(keys_ref, n_chunks, lambda kk, ci: kk > thr)
    cnt_ge = _count(keys_ref, n_chunks, lambda kk, ci: kk >= thr)
    tie_ref[...] = jnp.full((1, nq), INT_MAX, I32)

    @pl.when(jnp.max(cnt_ge) > kf)
    def _():
        need = kf - cnt_gt
        row = lax.broadcasted_iota(I32, (kc, nq), 0)
        n_bits = max(1, (keys_ref.shape[0] * kc - 1).bit_length())

        def idx_step(it, j):
            cand = j | lax.shift_left(I32(1), n_bits - 1 - it)
            cnt = _count(keys_ref, n_chunks,
                         lambda kk, ci: jnp.where(kk == thr, ci * kc + row, INT_MAX) < cand)
            return jnp.where(cnt < need, cand, j)

        tie_ref[...] = lax.fori_loop(0, n_bits, idx_step, jnp.zeros((1, nq), I32))

    return thr


def _selected(kk, key_index, thr, tie):
    return jnp.logical_or(kk > thr, jnp.logical_and(kk == thr, key_index <= tie))


def _head_rows(x_ref, h):
    rows = x_ref[0, h * HEAD_DIM:(h + 1) * HEAD_DIM, :]
    parts = [jnp.zeros_like(rows)] * HEADS_PER_VREG
    parts[h % HEADS_PER_VREG] = rows
    return jnp.concatenate(parts, axis=0)


def _attn_kernel(qt_ref, qit_ref, wit_ref, k_ref, vt_ref, ki2_ref, bias0_ref, bias1_ref, o_ref,
                 keys_ref, qm_ref, qim_ref, m_ref, l_ref, alpha_ref, acc_ref, s_ref, p_ref, tie_ref, *, top_k):
    t = qt_ref.shape[2]
    i = pl.program_id(1)
    key_row = lax.broadcasted_iota(I32, (t, t), 0)
    query_col = lax.broadcasted_iota(I32, (t, t), 1)
    causal = key_row <= query_col

    for h in range(N_HEADS):
        qm_ref[h] = _head_rows(qt_ref, h)
        qim_ref[h] = _head_rows(qit_ref, h)
    w = wit_ref[0]

    def score_chunk(ci, diagonal):
        kic = ki2_ref[0, ci]
        sc = jnp.zeros((t, t), F32)
        for h in range(IDX_HEADS):
            sc = sc + w[h:h + 1, :] * jnp.maximum(_dot(kic, qim_ref[h]), 0.0)
        if diagonal:
            sc = jnp.where(causal, sc, NEG_INF)
        keys_ref[ci] = _sortable(sc)

    def score_body(ci, carry):
        score_chunk(ci, False)
        return carry

    lax.fori_loop(0, i, score_body, 0)
    score_chunk(i, True)

    thr = _select_threshold(keys_ref, i + 1, top_k, tie_ref)
    tie = tie_ref[...]

    m_ref[...] = jnp.full(m_ref.shape, NEG_INF, F32)
    l_ref[...] = jnp.zeros(l_ref.shape, F32)
    acc_ref[...] = jnp.zeros(acc_ref.shape, F32)

    def attend_chunk(ci, mode):
        sel = _selected(keys_ref[ci], ci * t + key_row, thr, tie)
        if mode == "diag":
            sel = jnp.logical_and(sel, causal)
        mask_add = jnp.where(sel, 0.0, NEG_INF)
        kc = k_ref[0, ci]
        vc = vt_ref[0, ci]
        for h in range(N_HEADS):
            pair = slice((h // HEADS_PER_VREG) * LANES, (h // HEADS_PER_VREG + 1) * LANES)
            s = _dot(kc[:, pair], qm_ref[h])
            if mode == "far":
                s = s + mask_add
            elif mode == "prev":
                s = s + (bias1_ref[h] + mask_add)
            else:
                s = s + (bias0_ref[h] + mask_add)
            s_ref[h] = s
            m_old = m_ref[h]
            m_new = jnp.maximum(m_old, jnp.max(s, axis=0, keepdims=True))
            alpha_ref[h] = jnp.exp2(m_old - m_new)
            m_ref[h] = m_new
        for h in range(N_HEADS):
            p = jnp.exp2(s_ref[h] - m_ref[h])
            l_ref[h] = alpha_ref[h] * l_ref[h] + jnp.sum(p, axis=0, keepdims=True)
            p_ref[h] = p.astype(BF16)
        for h in range(N_HEADS):
            pair = slice((h // HEADS_PER_VREG) * LANES, (h // HEADS_PER_VREG + 1) * LANES)
            acc_ref[h] = alpha_ref[h] * acc_ref[h] + _dot(vc[pair, :], p_ref[h])

    def far_body(ci, carry):
        attend_chunk(ci, "far")
        return carry

    lax.fori_loop(0, i - 1, far_body, 0)

    @pl.when(i >= 1)
    def _():
        attend_chunk(i - 1, "prev")

    attend_chunk(i, "diag")

    for pr in range(N_HEADS // HEADS_PER_VREG):
        parts = []
        for hh in range(HEADS_PER_VREG):
            h = pr * HEADS_PER_VREG + hh
            parts.append(acc_ref[h, hh * HEAD_DIM:(hh + 1) * HEAD_DIM, :] / l_ref[h])
        o_ref[:, pr * LANES:(pr + 1) * LANES] = jnp.concatenate(parts, axis=0).T.astype(BF16)


def _prompt_attention(qt, qit, wit, kb, vtb4, ki2, bias0, bias1, top_k):
    batch, n_chunks, a, t = vtb4.shape
    seq = n_chunks * t
    qmap = lambda b, i: (b, 0, i)
    bmap = lambda b, i: (b, 0, 0, 0)
    cmap = lambda b, i: (0, 0, 0)
    kb4 = kb.reshape(batch, n_chunks, t, a)
    ki24 = ki2.reshape(batch, n_chunks, t, LANES)
    return pl.pallas_call(
        functools.partial(_attn_kernel, top_k=top_k),
        grid=(batch, n_chunks),
        in_specs=[pl.BlockSpec((1, a, t), qmap),
                  pl.BlockSpec((1, a, t), qmap),
                  pl.BlockSpec((1, IDX_HEADS, t), qmap),
                  _resident((1, n_chunks, t, a), bmap),
                  _resident((1, n_chunks, a, t), bmap),
                  _resident((1, n_chunks, t, LANES), bmap),
                  _resident((N_HEADS, t, t), cmap),
                  _resident((N_HEADS, t, t), cmap)],
        out_specs=pl.BlockSpec((t, a), lambda b, i: (b * n_chunks + i, 0)),
        out_shape=jax.ShapeDtypeStruct((batch * seq, a), BF16),
        scratch_shapes=[pltpu.VMEM((n_chunks, t, t), I32),
                        pltpu.VMEM((N_HEADS, LANES, t), BF16),
                        pltpu.VMEM((IDX_HEADS, LANES, t), BF16),
                        pltpu.VMEM((N_HEADS, 1, t), F32),
                        pltpu.VMEM((N_HEADS, 1, t), F32),
                        pltpu.VMEM((N_HEADS, 1, t), F32),
                        pltpu.VMEM((N_HEADS, LANES, t), F32),
                        pltpu.VMEM((N_HEADS, t, t), F32),
                        pltpu.VMEM((N_HEADS, t, t), BF16),
                        pltpu.VMEM((1, t), I32)],
        compiler_params=_cparams(2),
        name="prompt_attn",
    )(qt, qit, wit, kb4, vtb4, ki24, bias0, bias1)


def _bias_of_dist(dist, rb_ref, h):
    n = jnp.maximum(dist, 0)
    max_exact = N_BUCKETS // 2
    nf = jnp.maximum(n, 1).astype(F32)
    large = max_exact + (jnp.log(nf / max_exact) / math.log(REL_MAX_DIST / max_exact)
                         * (N_BUCKETS - max_exact)).astype(I32)
    large = jnp.minimum(large, N_BUCKETS - 1)
    bucket = jnp.where(n < max_exact, n, large)
    out = jnp.zeros(dist.shape, F32)
    for nb in range(N_BUCKETS):
        out = jnp.where(bucket == nb, rb_ref[nb, h], out)
    return (out - rb_ref[N_BUCKETS - 1, h]) * LOG2E


def _bias_kernel(rb_ref, b0_ref, b1_ref, bs_ref, *, t, past):
    key_row = lax.broadcasted_iota(I32, (t, t), 0)
    query_col = lax.broadcasted_iota(I32, (t, t), 1)
    key_lane = lax.broadcasted_iota(I32, (1, bs_ref.shape[1]), 1)
    for h in range(N_HEADS):
        b0_ref[h] = _bias_of_dist(query_col - key_row, rb_ref, h)
        b1_ref[h] = _bias_of_dist(t + query_col - key_row, rb_ref, h)
        bs_ref[h:h + 1, :] = _bias_of_dist(past - key_lane, rb_ref, h)


def _bias_tables(rel_bias, t, past, n_keys_s):
    return pl.pallas_call(
        functools.partial(_bias_kernel, t=t, past=past),
        in_specs=[pl.BlockSpec(memory_space=pltpu.SMEM)],
        out_shape=[jax.ShapeDtypeStruct((N_HEADS, t, t), F32),
                   jax.ShapeDtypeStruct((N_HEADS, t, t), F32),
                   jax.ShapeDtypeStruct((N_HEADS, n_keys_s), F32)],
        compiler_params=pltpu.CompilerParams(vmem_limit_bytes=VMEM_LIMIT_BYTES),
        name="bias_tables",
    )(rel_bias)


def _page_specs(layer, n_pages, rows, page):
    def spec(p):
        return pl.BlockSpec((1, 1, rows, page), lambda b, pt: (layer, pt[b * n_pages + p], 0, 0))
    return [spec(p) for p in range(n_pages)]


def _sscore_kernel(pt_ref, qi_ref, wi_ref, *refs):
    ik_refs, o_ref = refs[:-1], refs[-1]
    page = ik_refs[0].shape[3]
    for p, ik_ref in enumerate(ik_refs):
        dots = _dot(qi_ref[0], ik_ref[0, 0].astype(BF16))
        o_ref[0, :, p * page:(p + 1) * page] = jnp.sum(wi_ref[0] * jnp.maximum(dots, 0.0),
                                                       axis=0, keepdims=True)


def _sample_scores(layer, page_table_flat, qi8, wi8, idx_t, n_pages):
    db = qi8.shape[0]
    page = idx_t.shape[3]
    per_sample = lambda b, pt: (b, 0, 0)
    return pl.pallas_call(
        _sscore_kernel,
        grid_spec=pltpu.PrefetchScalarGridSpec(
            num_scalar_prefetch=1,
            grid=(db,),
            in_specs=[pl.BlockSpec((1, IDX_HEADS, IDX_DIM), per_sample),
                      pl.BlockSpec((1, IDX_HEADS, 1), per_sample)]
                     + _page_specs(layer, n_pages, IDX_DIM, page),
            out_specs=pl.BlockSpec((1, 1, n_pages * page), per_sample)),
        out_shape=jax.ShapeDtypeStruct((db, 1, n_pages * page), F32),
        compiler_params=_cparams(1),
        name="sample_scores",
    )(page_table_flat, qi8, wi8, *([idx_t] * n_pages))


def _sselect_kernel(sc_ref, qi_ref, kin_ref, wi_ref, o_ref, keys_ref, tie_ref, *, top_k, past):
    n_chunks, page, db = keys_ref.shape
    n_pages = n_chunks - 1
    for p in range(n_pages):
        keys_ref[p] = _sortable(sc_ref[:, p * page:(p + 1) * page].T)
    kin = kin_ref[...].astype(BF16).astype(F32)
    qif = qi_ref[...].astype(F32)
    w = wi_ref[...]
    s_new = jnp.zeros((db, 1), F32)
    for h in range(IDX_HEADS):
        qh = qif[:, h * IDX_DIM:(h + 1) * IDX_DIM]
        s_new = s_new + w[:, h:h + 1] * jnp.maximum(jnp.sum(qh * kin, axis=-1, keepdims=True), 0.0)
    lane = lax.broadcasted_iota(I32, (db, page), 1)
    keys_ref[n_pages] = _sortable(jnp.where(lane == 0, s_new, NEG_INF).T)

    thr = _select_threshold(keys_ref, n_chunks, top_k, tie_ref)
    tie = tie_ref[...]
    row = lax.broadcasted_iota(I32, (page, db), 0)
    for p in range(n_chunks):
        key_index = p * page + row
        sel = jnp.logical_and(key_index <= past, _selected(keys_ref[p], key_index, thr, tie))
        o_ref[:, p * page:(p + 1) * page] = jnp.where(sel, 0.0, NEG_INF).T


def _sample_select(scores, qi, ki_new, wi, top_k, past, page):
    db = scores.shape[0]
    n_chunks = past // page + 1
    return pl.pallas_call(
        functools.partial(_sselect_kernel, top_k=top_k, past=past),
        out_shape=jax.ShapeDtypeStruct((db, n_chunks * page), F32),
        scratch_shapes=[pltpu.VMEM((n_chunks, page, db), I32),
                        pltpu.VMEM((1, db), I32)],
        compiler_params=pltpu.CompilerParams(vmem_limit_bytes=VMEM_LIMIT_BYTES),
        name="sample_select",
    )(scores, qi, ki_new, wi)


def _sattn_kernel(pt_ref, q_ref, mask_ref, bias_ref, kn_ref, vn_ref, *refs):
    n_pages = (len(refs) - 1) // 2
    k_refs, v_refs, o_ref = refs[:n_pages], refs[n_pages:2 * n_pages], refs[-1]
    a = q_ref.shape[2]
    page = k_refs[0].shape[3]
    q = q_ref[0]
    lane_head = lax.shift_right_logical(lax.broadcasted_iota(I32, (N_HEADS, a), 1),
                                        HEAD_DIM.bit_length() - 1)
    own = lane_head == lax.broadcasted_iota(I32, (N_HEADS, a), 0)
    qf = q.astype(F32)
    q_heads = jnp.where(own, qf, 0.0).astype(BF16)
    parts = [_dot(q_heads, k_ref[0, 0].astype(BF16)) for k_ref in k_refs]
    kn = kn_ref[0].astype(BF16).astype(F32)
    g_new = jnp.sum(jnp.where(own, qf * kn, 0.0), axis=-1, keepdims=True)
    parts.append(jnp.where(lax.broadcasted_iota(I32, (N_HEADS, page), 1) == 0, g_new, 0.0))
    s = jnp.concatenate(parts, axis=1) + bias_ref[...] + mask_ref[0]
    p = jnp.exp2(s - jnp.max(s, axis=-1, keepdims=True))
    l = jnp.sum(p, axis=-1, keepdims=True)
    new_col = n_pages * page
    acc = p[:, new_col:new_col + 1] * vn_ref[0].astype(BF16).astype(F32)
    for pg, v_ref in enumerate(v_refs):
        acc = acc + _dot_nt(p[:, pg * page:(pg + 1) * page].astype(BF16), v_ref[0, 0].astype(BF16))
    o_ref[0] = jnp.sum(jnp.where(own, acc / l, 0.0), axis=0, keepdims=True).astype(BF16)


def _sample_attention(layer, page_table_flat, q3, mask3, bias_s, k_new3, v_new3, kt_cache, vt_cache, n_pages):
    db, _, a = q3.shape
    page = kt_cache.shape[3]
    n_keys = mask3.shape[2]
    per_sample = lambda b, pt: (b, 0, 0)
    return pl.pallas_call(
        _sattn_kernel,
        grid_spec=pltpu.PrefetchScalarGridSpec(
            num_scalar_prefetch=1,
            grid=(db,),
            in_specs=[pl.BlockSpec((1, 1, a), per_sample),
                      pl.BlockSpec((1, 1, n_keys), per_sample),
                      _resident((N_HEADS, n_keys), lambda b, pt: (0, 0)),
                      pl.BlockSpec((1, 1, a), per_sample),
                      pl.BlockSpec((1, 1, a), per_sample)]
                     + _page_specs(layer, n_pages, a, page) + _page_specs(layer, n_pages, a, page),
            out_specs=pl.BlockSpec((1, 1, a), per_sample)),
        out_shape=jax.ShapeDtypeStruct((db, 1, a), BF16),
        compiler_params=_cparams(1),
        name="sample_attn",
    )(page_table_flat, q3, mask3, bias_s, k_new3, v_new3, *([kt_cache] * n_pages), *([vt_cache] * n_pages))


def _merge_kernel(h_ref, z_ref, zp1_ref, zp2_ref, prev_ref, bc_ref, sgc_ref, sga_ref, o_ref, cw_ref,
                  woc_ref, woa_ref, wout_ref, x_ref, *, tiles_per_seq, sequence):
    z = z_ref[...]
    if sequence:
        first = (pl.program_id(0) % tiles_per_seq) == 0
        before1 = jnp.where(first, prev_ref[0, 1:2, :], zp1_ref[7:8, :])
        before2 = jnp.where(first, prev_ref[0, 0:1, :], zp1_ref[6:7, :])
        rowi = lax.broadcasted_iota(I32, z.shape, 0)
        z1 = jnp.where(rowi == 0, before1, pltpu.roll(z, 1, 0))
        z2 = jnp.where(rowi == 0, before2, jnp.where(rowi == 1, before1, pltpu.roll(z, 2, 0)))
    else:
        z1 = zp1_ref[...]
        z2 = zp2_ref[...]
    conv = cw_ref[0:1, :] * z2 + cw_ref[1:2, :] * z1 + cw_ref[2:3, :] * z
    y_conv = _dot((bc_ref[...] * conv).astype(BF16), woc_ref[...])
    y_attn = _dot(o_ref[...], woa_ref[...])
    merged = sgc_ref[...] * y_conv + sga_ref[...] * y_attn
    x_ref[...] = h_ref[...] + _dot(merged.astype(BF16), wout_ref[...])


def _merge(h, z, zp1, zp2, prev, bc, sgc, sga, o, cw, woc, woa, wout, batch, sequence):
    m, d = h.shape
    c = z.shape[1]
    seq = m // batch
    tm = min(ROW_TILE, seq) if sequence else min(ROW_TILE, m)
    tiles_per_seq = seq // tm if sequence else 1
    row = lambda i: (i, 0)
    const = lambda i: (0, 0)
    if sequence:
        halo = tm // SUBLANES
        zp1_spec = pl.BlockSpec((SUBLANES, c), lambda i: (jnp.maximum(i * halo - 1, 0), 0))
        zp2_spec = pl.BlockSpec((SUBLANES, c), lambda i: (0, 0))
        prev_spec = pl.BlockSpec((1, CONV_W - 1, c), lambda i: (i // tiles_per_seq, 0, 0))
    else:
        zp1_spec = pl.BlockSpec((tm, c), row)
        zp2_spec = pl.BlockSpec((tm, c), row)
        prev_spec = pl.BlockSpec((1, CONV_W - 1, c), lambda i: (0, 0, 0))
    return pl.pallas_call(
        functools.partial(_merge_kernel, tiles_per_seq=tiles_per_seq, sequence=sequence),
        grid=(m // tm,),
        in_specs=[pl.BlockSpec((tm, d), row),
                  pl.BlockSpec((tm, c), row),
                  zp1_spec, zp2_spec, prev_spec,
                  pl.BlockSpec((tm, c), row),
                  pl.BlockSpec((tm, d), row),
                  pl.BlockSpec((tm, d), row),
                  pl.BlockSpec((tm, o.shape[1]), row),
                  _resident((CONV_W, c), const),
                  _resident(woc.shape, const),
                  _resident(woa.shape, const),
                  _resident(wout.shape, const)],
        out_specs=pl.BlockSpec((tm, d), row),
        out_shape=jax.ShapeDtypeStruct((m, d), F32),
        compiler_params=_cparams(1),
        name="merge",
    )(h, z, zp1, zp2, prev, bc, sgc, sga, o, cw, woc, woa, wout)


def _arrange_w_in(w, d):
    c = a = d // 2
    o = 3 * c + 3 * a + IDX_HEADS * IDX_DIM
    head = w[:, :o]
    ki = jnp.pad(w[:, o:o + IDX_DIM], ((0, 0), (0, LANES - IDX_DIM)))
    wi = jnp.pad(w[:, o + IDX_DIM:o + IDX_DIM + IDX_HEADS], ((0, 0), (0, LANES - IDX_HEADS)))
    gates = w[:, o + IDX_DIM + IDX_HEADS:]
    return jnp.concatenate([head, ki, wi, gates], axis=1).astype(BF16)


def kernel(x_prompt, x_sample, cache_k, cache_v, cache_idx_k, state_conv, page_table, rel_bias,
           norm_ffn1, w_ffn1_in, w_ffn1_out, norm_mix, w_in, conv_w, w_o_conv, w_o_attn, w_out,
           norm_ffn2, w_ffn2_in, w_ffn2_out, norm_final):
    batch, seq, d = x_prompt.shape
    db, dec_seq, _ = x_sample.shape
    assert dec_seq == 1
    depth = w_in.shape[0]
    c = a = d // 2
    n_pool, page = cache_k.shape[1:3]
    n_pages = page_table.shape[1]
    past = n_pages * page
    n_keys_s = past + page
    t = min(ATT_TILE, seq)
    assert seq % t == 0 and seq % min(ROW_TILE, seq) == 0 and t >= REL_MAX_DIST
    top_k_p = min(TOPK_MAX, seq // 4)
    top_k_s = min(TOPK_MAX, (past + dec_seq) // 4)

    bias0, bias1, bias_s = _bias_tables(rel_bias, t, past, n_keys_s)
    pt_flat = page_table.reshape(-1)
    g_final = norm_final.reshape(1, d)
    conv_zero = jnp.zeros((batch, CONV_W - 1, c), F32)
    kt_cache = jnp.transpose(cache_k, (0, 1, 3, 4, 2)).reshape(depth, n_pool, a, page)
    vt_cache = jnp.transpose(cache_v, (0, 1, 3, 4, 2)).reshape(depth, n_pool, a, page)
    idx_t = jnp.transpose(cache_idx_k, (0, 1, 3, 2))

    xp = x_prompt.reshape(batch * seq, d)
    xs = x_sample.reshape(db, d)
    outs_p = {"k": [], "v": [], "ik": [], "conv": []}
    outs_s = {"k": [], "v": [], "ik": [], "conv": []}

    for l in range(depth):
        last = l == depth - 1
        g1 = norm_ffn1[l].reshape(1, d)
        g2 = norm_ffn2[l].reshape(1, d)
        gm = norm_mix[l].reshape(1, d)
        w1_in, w1_out = w_ffn1_in[l].astype(BF16), w_ffn1_out[l].astype(BF16)
        w2_in, w2_out = w_ffn2_in[l].astype(BF16), w_ffn2_out[l].astype(BF16)
        wp = _arrange_w_in(w_in[l], d)
        woc, woa, wout = w_o_conv[l].astype(BF16), w_o_attn[l].astype(BF16), w_out[l].astype(BF16)
        cw = conv_w[l]

        hp = _ffn(xp, g1, w1_in, w1_out, g_final, False)
        (z, bc, sgc, sga, kt, vt, kit, kb, ki2, vtb, qt, qit, wit) = _proj(hp, gm, wp, batch, True, t)
        o = _prompt_attention(qt, qit, wit, kb, vtb, ki2, bias0, bias1, top_k_p)
        x2 = _merge(hp, z, z, z, conv_zero, bc, sgc, sga, o, cw, woc, woa, wout, batch, True)
        xp = _ffn(x2, g2, w2_in, w2_out, g_final, last)
        outs_p["k"].append(kt)
        outs_p["v"].append(vt)
        outs_p["ik"].append(kit)
        outs_p["conv"].append(z.reshape(batch, seq, c)[:, seq - (CONV_W - 1):])

        hs = _ffn(xs, g1, w1_in, w1_out, g_final, False)
        (z, bc, sgc, sga, kt, vt, kit, q, qi, wi, k, v, ki) = _proj(hs, gm, wp, 1, False)
        scores = _sample_scores(l, pt_flat, qi.reshape(db, IDX_HEADS, IDX_DIM),
                                wi.reshape(db, IDX_HEADS, 1), idx_t, n_pages)
        mask = _sample_select(scores.reshape(db, past), qi, ki, wi, top_k_s, past, page)
        o = _sample_attention(l, pt_flat, q.reshape(db, 1, a), mask.reshape(db, 1, n_keys_s), bias_s,
                              k.reshape(db, 1, a), v.reshape(db, 1, a), kt_cache, vt_cache, n_pages)
        prev = state_conv[l]
        x2 = _merge(hs, z, prev[:, 1], prev[:, 0], conv_zero, bc, sgc, sga, o.reshape(db, a), cw,
                    woc, woa, wout, 1, False)
        xs = _ffn(x2, g2, w2_in, w2_out, g_final, last)
        outs_s["k"].append(kt)
        outs_s["v"].append(vt)
        outs_s["ik"].append(kit)
        outs_s["conv"].append(jnp.stack([prev[:, 1], z], axis=1))

    def heads_last(parts, lead, n_pos):
        x = jnp.stack(parts).reshape(depth, -1, N_HEADS, HEAD_DIM, n_pos)
        return jnp.transpose(x, (0, 1, 4, 2, 3)).reshape((depth,) + lead + (N_HEADS, HEAD_DIM))

    def dim_last(parts, lead, n_pos):
        x = jnp.transpose(jnp.stack(parts), (0, 1, 3, 2))
        return x.reshape((depth,) + lead + (IDX_DIM,))

    return (xp.reshape(batch, seq, d),
            xs.reshape(db, dec_seq, d),
            heads_last(outs_p["k"], (batch, seq), seq),
            heads_last(outs_p["v"], (batch, seq), seq),
            dim_last(outs_p["ik"], (batch, seq), seq),
            jnp.stack(outs_p["conv"]),
            heads_last(outs_s["k"], (db, dec_seq), db),
            heads_last(outs_s["v"], (db, dec_seq), db),
            dim_last(outs_s["ik"], (db, dec_seq), db),
            jnp.stack(outs_s["conv"]))
```

```python
import functools
import math

import jax
import jax.numpy as jnp
from jax import lax
from jax.experimental import pallas as pl
from jax.experimental.pallas import tpu as pltpu

F32 = jnp.float32
BF16 = jnp.bfloat16
I32 = jnp.int32

EPS = 1e-6
NEG_INF = -1e30
HEAD_DIM = 64
N_HEADS = 8
IDX_HEADS = 8
IDX_DIM = 64
TOPK_MAX = 256
N_BUCKETS = 32
REL_MAX_DIST = 128
CONV_W = 3
LANES = 128
SUBLANES = 8
HEADS_PER_VREG = LANES // HEAD_DIM
VMEM_LIMIT_BYTES = 56 * 1024 * 1024
ROW_TILE = 512
ATT_TILE = 256
FFN_CHUNK = 256
INT_MAX = 2**31 - 1
LOG2E = math.log2(math.e)
I16 = jnp.int16
PACKED_ROWS = 2 * SUBLANES
HALF_BITS = 16
HALF_MASK = 2**HALF_BITS - 1
HALF_BIAS = 2**(HALF_BITS - 1)


def _cparams(n_axes):
    return pltpu.CompilerParams(dimension_semantics=("arbitrary",) * n_axes,
                                vmem_limit_bytes=VMEM_LIMIT_BYTES)


def _resident(shape, index_map):
    return pl.BlockSpec(shape, index_map, pipeline_mode=pl.Buffered(1))


def _rms(x, g):
    return x * lax.rsqrt(jnp.mean(x * x, axis=-1, keepdims=True) + EPS) * g


def _sigmoid(x):
    return 1.0 / (1.0 + jnp.exp(-x))


def _dot(a, b):
    return jnp.dot(a, b, preferred_element_type=F32)


def _dot_nt(a, b):
    return lax.dot_general(a, b, (((1,), (1,)), ((), ())), preferred_element_type=F32)


def _ffn_kernel(x_ref, g_ref, wgu_ref, wd_ref, gf_ref, o_ref, acc_ref, *, d_ff, final_norm):
    x = x_ref[...]
    xb = _rms(x, g_ref[...]).astype(BF16)
    for j in range(d_ff // FFN_CHUNK):
        lo = j * FFN_CHUNK
        a = _dot(xb, wgu_ref[:, lo:lo + FFN_CHUNK])
        b = _dot(xb, wgu_ref[:, d_ff + lo:d_ff + lo + FFN_CHUNK])
        hid = (a * _sigmoid(a) * b).astype(BF16)
        part = _dot(hid, wd_ref[lo:lo + FFN_CHUNK, :])
        if j == 0:
            acc_ref[...] = part
        else:
            acc_ref[...] += part
    y = x + 0.5 * acc_ref[...]
    if final_norm:
        y = _rms(y, gf_ref[...])
    o_ref[...] = y


def _ffn(x, g, wgu, wd, g_final, final_norm):
    m, d = x.shape
    d_ff = wd.shape[0]
    tm = min(ROW_TILE, m)
    return pl.pallas_call(
        functools.partial(_ffn_kernel, d_ff=d_ff, final_norm=final_norm),
        grid=(m // tm,),
        in_specs=[pl.BlockSpec((tm, d), lambda i: (i, 0)),
                  _resident((1, d), lambda i: (0, 0)),
                  _resident((d, 2 * d_ff), lambda i: (0, 0)),
                  _resident((d_ff, d), lambda i: (0, 0)),
                  _resident((1, d), lambda i: (0, 0))],
        out_specs=pl.BlockSpec((tm, d), lambda i: (i, 0)),
        out_shape=jax.ShapeDtypeStruct((m, d), F32),
        scratch_shapes=[pltpu.VMEM((tm, d), F32)],
        compiler_params=_cparams(1),
        name="ffn",
    )(x, g, wgu, wd, g_final)


def _proj_kernel(h_ref, g_ref, w_ref, z_ref, bc_ref, sgc_ref, sga_ref, kt_ref, vt_ref, kit_ref, *refs,
                 c, a, d, prompt):
    u = _rms(h_ref[...], g_ref[...]).astype(BF16)

    def mm(lo, hi):
        return _dot(u, w_ref[:, lo:hi])

    z_ref[...] = mm(2 * c, 3 * c) * mm(0, c)
    bc_ref[...] = mm(c, 2 * c)
    o = 3 * c
    q = mm(o, o + a)
    k = mm(o + a, o + 2 * a)
    v = mm(o + 2 * a, o + 3 * a)
    o += 3 * a
    qi = mm(o, o + IDX_HEADS * IDX_DIM) * IDX_DIM ** -0.5
    o += IDX_HEADS * IDX_DIM
    ki = mm(o, o + LANES)
    wi = mm(o + LANES, o + 2 * LANES) * IDX_HEADS ** -0.5
    o += 2 * LANES
    sgc_ref[...] = _sigmoid(mm(o, o + d))
    sga_ref[...] = _sigmoid(mm(o + d, o + 2 * d))
    kt = k.T
    vt = v.T
    kt_ref[0] = kt
    vt_ref[0] = vt
    kit_ref[0] = ki.T[:IDX_DIM, :]
    if prompt:
        kb_ref, ki2_ref, vtb_ref, qt_ref, qit_ref, wit_ref = refs
        kb_ref[...] = k.astype(BF16)
        ki2_ref[...] = (ki + pltpu.roll(ki, IDX_DIM, 1)).astype(BF16)
        _, n_sub, n_pairs, v_rows, t = vtb_ref.shape
        for j in range(n_sub):
            for pr in range(n_pairs):
                vtb_ref[0, j, pr, :LANES, :] = vt[pr * LANES:(pr + 1) * LANES, j * t:(j + 1) * t].astype(BF16)
                vtb_ref[0, j, pr, LANES:, :] = jnp.ones((v_rows - LANES, t), BF16)
        qt_ref[0] = (q * (HEAD_DIM ** -0.5 * LOG2E)).T.astype(BF16)
        qit_ref[0] = qi.T.astype(BF16)
        wit_ref[0] = wi.T[:IDX_HEADS, :]
    else:
        q_ref, qi_ref, wi_ref, k_ref, v_ref, ki_ref = refs
        q_ref[...] = (q * (HEAD_DIM ** -0.5 * LOG2E)).astype(BF16)
        qi_ref[...] = qi.astype(BF16)
        wi_ref[...] = wi[:, :IDX_HEADS]
        k_ref[...] = k
        v_ref[...] = v
        ki_ref[...] = ki[:, :IDX_DIM]


def _proj(h, g, w, batch, prompt, att_tile=None):
    m, d = h.shape
    n_cols = w.shape[1]
    c = a = d // 2
    seq = m // batch
    tm = min(ROW_TILE, seq)
    tiles_per_seq = seq // tm
    row = lambda i: (i, 0)
    fmaj = lambda i: (i // tiles_per_seq, 0, i % tiles_per_seq)
    chunked = lambda i: (i // tiles_per_seq, i % tiles_per_seq, 0, 0)

    def nat(cols, dtype):
        return jax.ShapeDtypeStruct((m, cols), dtype), pl.BlockSpec((tm, cols), row)

    def fm(rows, dtype):
        return jax.ShapeDtypeStruct((batch, rows, seq), dtype), pl.BlockSpec((1, rows, tm), fmaj)

    outs = [nat(c, F32),
            nat(c, F32),
            nat(d, F32),
            nat(d, F32),
            fm(a, F32),
            fm(a, F32),
            fm(IDX_DIM, F32)]
    if prompt:
        assert tm % att_tile == 0
        outs += [nat(a, BF16),
                 nat(LANES, BF16),
                 (jax.ShapeDtypeStruct((batch, seq // att_tile, a // LANES, LANES + PACKED_ROWS, att_tile), BF16),
                  pl.BlockSpec((1, tm // att_tile, a // LANES, LANES + PACKED_ROWS, att_tile),
                               lambda i: (i // tiles_per_seq, i % tiles_per_seq, 0, 0, 0))),
                 fm(a, BF16),
                 fm(a, BF16),
                 fm(IDX_HEADS, F32)]
    else:
        outs += [nat(a, BF16), nat(a, BF16), nat(IDX_HEADS, F32), nat(a, F32), nat(a, F32), nat(IDX_DIM, F32)]
    return pl.pallas_call(
        functools.partial(_proj_kernel, c=c, a=a, d=d, prompt=prompt),
        grid=(m // tm,),
        in_specs=[pl.BlockSpec((tm, d), row),
                  _resident((1, d), lambda i: (0, 0)),
                  _resident((d, n_cols), lambda i: (0, 0))],
        out_specs=[o[1] for o in outs],
        out_shape=[o[0] for o in outs],
        compiler_params=_cparams(1),
        name="proj",
    )(h, g, w)


def _sortable(x):
    x = jnp.where(x == 0.0, 0.0, x)
    b = lax.bitcast_convert_type(x, I32)
    return b ^ ((b >> 31) & I32(INT_MAX))


def _store_keys(keys_ref, hi_ref, lo_ref, ci, scores):
    k = _sortable(scores)
    keys_ref[ci] = k
    hi_ref[ci] = (k >> HALF_BITS).astype(I16)
    lo_ref[ci] = ((k & HALF_MASK) - HALF_BIAS).astype(I16)


def _count(keys_ref, n_chunks, pred):
    kc, nq = keys_ref.shape[1:]

    def body(ci, acc):
        hit = jnp.where(pred(keys_ref[ci], ci), 1.0, 0.0)
        return acc + jnp.sum(hit.reshape(kc // SUBLANES, SUBLANES, nq), axis=0)

    acc = lax.fori_loop(0, n_chunks, body, jnp.zeros((SUBLANES, nq), F32))
    return jnp.sum(acc, axis=0, keepdims=True)


def _count16(x_ref, n_chunks, bound, strict=False):
    kc, nq = x_ref.shape[1:]
    b16 = bound.astype(I16)

    def body(j, acc):
        for ci in (2 * j, 2 * j + 1):
            x = x_ref[ci]
            hit = jnp.where(x > b16 if strict else x >= b16, jnp.int16(1), jnp.int16(0))
            for r in range(kc // PACKED_ROWS):
                acc = acc + hit[r * PACKED_ROWS:(r + 1) * PACKED_ROWS, :]
        return acc

    acc = lax.fori_loop(0, (n_chunks + 1) // 2, body, jnp.zeros((PACKED_ROWS, nq), I16))
    return jnp.sum(acc.astype(I32), axis=0, keepdims=True).astype(F32)


def _park_chunk(hi_ref, lo_ref, ci):
    hi_ref[ci] = jnp.full(hi_ref.shape[1:], -HALF_BIAS, I16)
    lo_ref[ci] = jnp.full(lo_ref.shape[1:], -HALF_BIAS, I16)


def _radix16(x_ref, n_chunks, rank):
    nq = x_ref.shape[2]

    def bit_step(it, prefix):
        cand = prefix | lax.shift_left(I32(1), HALF_BITS - 1 - it)
        cnt = _count16(x_ref, n_chunks, cand - HALF_BIAS)
        return jnp.where(cnt >= rank, cand, prefix)

    return lax.fori_loop(0, HALF_BITS, bit_step, jnp.zeros((1, nq), I32))


def _select_threshold(keys_ref, hi_ref, lo_ref, n_chunks, top_k, tie_ref):
    _, kc, nq = keys_ref.shape
    kf = float(top_k)

    hi = _radix16(hi_ref, n_chunks, kf) - HALF_BIAS
    above = _count16(hi_ref, n_chunks, hi, strict=True)
    hi16 = hi.astype(I16)

    def keep_equal_high(ci, carry):
        lo_ref[ci] = jnp.where(hi_ref[ci] == hi16, lo_ref[ci], jnp.int16(-HALF_BIAS))
        return carry

    lax.fori_loop(0, n_chunks, keep_equal_high, 0)
    lo_bits = _radix16(lo_ref, n_chunks, kf - above)
    thr = jnp.left_shift(hi, HALF_BITS) | lo_bits
    cnt_gt = above + _count16(lo_ref, n_chunks, lo_bits - HALF_BIAS, strict=True)
    cnt_ge = above + jnp.where(lo_bits == 0, _count16(hi_ref, n_chunks, hi) - above,
                               _count16(lo_ref, n_chunks, lo_bits - HALF_BIAS))
    tie_ref[...] = jnp.full((1, nq), INT_MAX, I32)

    excess_ties = jnp.max(cnt_ge) > kf

    @pl.when(excess_ties)
    def _():
        need = kf - cnt_gt
        row = lax.broadcasted_iota(I32, (kc, nq), 0)
        n_bits = max(1, (keys_ref.shape[0] * kc - 1).bit_length())

        def idx_step(it, j):
            cand = j | lax.shift_left(I32(1), n_bits - 1 - it)
            cnt = _count(keys_ref, n_chunks,
                         lambda kk, ci: jnp.where(kk == thr, ci * kc + row, INT_MAX) < cand)
            return jnp.where(cnt < need, cand, j)

        tie_ref[...] = lax.fori_loop(0, n_bits, idx_step, jnp.zeros((1, nq), I32))

    return thr, excess_ties


def _selected(kk, key_index, thr, tie):
    return jnp.logical_or(kk > thr, jnp.logical_and(kk == thr, key_index <= tie))


def _head_rows(x_ref, h):
    rows = x_ref[0, h * HEAD_DIM:(h + 1) * HEAD_DIM, :]
    parts = [jnp.zeros_like(rows)] * HEADS_PER_VREG
    parts[h % HEADS_PER_VREG] = rows
    return jnp.concatenate(parts, axis=0)


def _attn_kernel(qt_ref, qit_ref, wit_ref, k_ref, vt_ref, ki2_ref, bias0_ref, bias1_ref, o_ref,
                 keys_ref, hi_ref, lo_ref, qm_ref, qim_ref, m_ref, alpha_ref, acc_ref, mask_ref, s_ref, p_ref,
                 tie_ref, *, top_k):
    t = qt_ref.shape[2]
    i = pl.program_id(1)
    key_row = lax.broadcasted_iota(I32, (t, t), 0)
    query_col = lax.broadcasted_iota(I32, (t, t), 1)
    causal = key_row <= query_col

    for h in range(N_HEADS):
        qm_ref[h] = _head_rows(qt_ref, h)
        qim_ref[h] = _head_rows(qit_ref, h)
    w = wit_ref[0]

    def score_chunk(ci, diagonal):
        kic = ki2_ref[0, ci]
        sc = jnp.zeros((t, t), F32)
        for h in range(IDX_HEADS):
            sc = sc + w[h:h + 1, :] * jnp.maximum(_dot(kic, qim_ref[h]), 0.0)
        if diagonal:
            sc = jnp.where(causal, sc, NEG_INF)
        _store_keys(keys_ref, hi_ref, lo_ref, ci, sc)

    def score_body(ci, carry):
        score_chunk(ci, False)
        return carry

    lax.fori_loop(0, i, score_body, 0)
    score_chunk(i, True)
    _park_chunk(hi_ref, lo_ref, i + 1)

    thr, _ = _select_threshold(keys_ref, hi_ref, lo_ref, i + 1, top_k, tie_ref)
    tie = tie_ref[...]

    m_ref[...] = jnp.full(m_ref.shape, NEG_INF, F32)
    acc_ref[...] = jnp.zeros(acc_ref.shape, F32)

    def attend_chunk(ci, mode):
        sel = _selected(keys_ref[ci], ci * t + key_row, thr, tie)
        if mode == "diag":
            sel = jnp.logical_and(sel, causal)
        mask_ref[...] = jnp.where(sel, 0.0, NEG_INF)
        kc = k_ref[0, ci]
        def logits(h):
            pair = slice((h // HEADS_PER_VREG) * LANES, (h // HEADS_PER_VREG + 1) * LANES)
            s = _dot(kc[:, pair], qm_ref[h])
            if mode == "far":
                s = s + mask_ref[...]
            elif mode == "prev":
                s = s + (bias1_ref[h] + mask_ref[...])
            else:
                s = s + (bias0_ref[h] + mask_ref[...])
            s_ref[h] = s
            m_old = m_ref[h]
            m_new = jnp.maximum(m_old, jnp.max(s, axis=0, keepdims=True))
            alpha_ref[h] = jnp.exp2(m_old - m_new)
            m_ref[h] = m_new

        def weights(h):
            p_ref[h] = jnp.exp2(s_ref[h] - m_ref[h]).astype(BF16)

        def accumulate(h):
            acc_ref[h] = alpha_ref[h] * acc_ref[h] + _dot(vt_ref[0, ci, h // HEADS_PER_VREG], p_ref[h])

        for stage in (logits, weights, accumulate):
            for h in range(N_HEADS):
                stage(h)

    def far_body(ci, carry):
        attend_chunk(ci, "far")
        return carry

    lax.fori_loop(0, i - 1, far_body, 0)

    @pl.when(i >= 1)
    def _():
        attend_chunk(i - 1, "prev")

    attend_chunk(i, "diag")

    for pr in range(N_HEADS // HEADS_PER_VREG):
        parts = []
        for hh in range(HEADS_PER_VREG):
            h = pr * HEADS_PER_VREG + hh
            parts.append(acc_ref[h, hh * HEAD_DIM:(hh + 1) * HEAD_DIM, :] / acc_ref[h, LANES:LANES + 1, :])
        o_ref[:, pr * LANES:(pr + 1) * LANES] = jnp.concatenate(parts, axis=0).T.astype(BF16)


def _prompt_attention(qt, qit, wit, kb, vtb5, ki2, bias0, bias1, top_k):
    batch, n_chunks, n_pairs, v_rows, t = vtb5.shape
    a = n_pairs * LANES
    seq = n_chunks * t
    qmap = lambda b, i: (b, 0, i)
    bmap = lambda b, i: (b, 0, 0, 0)
    cmap = lambda b, i: (0, 0, 0)
    kb4 = kb.reshape(batch, n_chunks, t, a)
    ki24 = ki2.reshape(batch, n_chunks, t, LANES)
    return pl.pallas_call(
        functools.partial(_attn_kernel, top_k=top_k),
        grid=(batch, n_chunks),
        in_specs=[pl.BlockSpec((1, a, t), qmap),
                  pl.BlockSpec((1, a, t), qmap),
                  pl.BlockSpec((1, IDX_HEADS, t), qmap),
                  _resident((1, n_chunks, t, a), bmap),
                  _resident((1, n_chunks, n_pairs, v_rows, t), lambda b, i: (b, 0, 0, 0, 0)),
                  _resident((1, n_chunks, t, LANES), bmap),
                  _resident((N_HEADS, t, t), cmap),
                  _resident((N_HEADS, t, t), cmap)],
        out_specs=pl.BlockSpec((t, a), lambda b, i: (b * n_chunks + i, 0)),
        out_shape=jax.ShapeDtypeStruct((batch * seq, a), BF16),
        scratch_shapes=[pltpu.VMEM((n_chunks, t, t), I32),
                        pltpu.VMEM((n_chunks + 1, t, t), I16),
                        pltpu.VMEM((n_chunks + 1, t, t), I16),
                        pltpu.VMEM((N_HEADS, LANES, t), BF16),
                        pltpu.VMEM((IDX_HEADS, LANES, t), BF16),
                        pltpu.VMEM((N_HEADS, 1, t), F32),
                        pltpu.VMEM((N_HEADS, 1, t), F32),
                        pltpu.VMEM((N_HEADS, v_rows, t), F32),
                        pltpu.VMEM((t, t), F32),
                        pltpu.VMEM((N_HEADS, t, t), F32),
                        pltpu.VMEM((N_HEADS, t, t), BF16),
                        pltpu.VMEM((1, t), I32)],
        compiler_params=_cparams(2),
        name="prompt_attn",
    )(qt, qit, wit, kb4, vtb5, ki24, bias0, bias1)


def _bias_of_dist(dist, rb_ref, h):
    n = jnp.maximum(dist, 0)
    max_exact = N_BUCKETS // 2
    nf = jnp.maximum(n, 1).astype(F32)
    large = max_exact + (jnp.log(nf / max_exact) / math.log(REL_MAX_DIST / max_exact)
                         * (N_BUCKETS - max_exact)).astype(I32)
    large = jnp.minimum(large, N_BUCKETS - 1)
    bucket = jnp.where(n < max_exact, n, large)
    out = jnp.zeros(dist.shape, F32)
    for nb in range(N_BUCKETS):
        out = jnp.where(bucket == nb, rb_ref[nb, h], out)
    return (out - rb_ref[N_BUCKETS - 1, h]) * LOG2E


def _bias_kernel(rb_ref, b0_ref, b1_ref, bs_ref, *, t, past):
    key_row = lax.broadcasted_iota(I32, (t, t), 0)
    query_col = lax.broadcasted_iota(I32, (t, t), 1)
    key_lane = lax.broadcasted_iota(I32, (1, bs_ref.shape[1]), 1)
    for h in range(N_HEADS):
        b0_ref[h] = _bias_of_dist(query_col - key_row, rb_ref, h)
        b1_ref[h] = _bias_of_dist(t + query_col - key_row, rb_ref, h)
        bs_ref[h:h + 1, :] = _bias_of_dist(past - key_lane, rb_ref, h)


def _bias_tables(rel_bias, t, past, n_keys_s):
    return pl.pallas_call(
        functools.partial(_bias_kernel, t=t, past=past),
        in_specs=[pl.BlockSpec(memory_space=pltpu.SMEM)],
        out_shape=[jax.ShapeDtypeStruct((N_HEADS, t, t), F32),
                   jax.ShapeDtypeStruct((N_HEADS, t, t), F32),
                   jax.ShapeDtypeStruct((N_HEADS, n_keys_s), F32)],
        compiler_params=pltpu.CompilerParams(vmem_limit_bytes=VMEM_LIMIT_BYTES),
        name="bias_tables",
    )(rel_bias)


def _page_specs(layer, n_pages, rows, page):
    def spec(p):
        return pl.BlockSpec((1, 1, rows, page), lambda b, pt: (layer, pt[b * n_pages + p], 0, 0))
    return [spec(p) for p in range(n_pages)]


def _sscore_kernel(pt_ref, qi_ref, wi_ref, *refs):
    ik_refs, o_ref = refs[:-1], refs[-1]
    page = ik_refs[0].shape[3]
    for p, ik_ref in enumerate(ik_refs):
        dots = _dot(qi_ref[0], ik_ref[0, 0].astype(BF16))
        o_ref[0, :, p * page:(p + 1) * page] = jnp.sum(wi_ref[0] * jnp.maximum(dots, 0.0),
                                                       axis=0, keepdims=True)


def _sample_scores(layer, page_table_flat, qi8, wi8, idx_t, n_pages):
    db = qi8.shape[0]
    page = idx_t.shape[3]
    per_sample = lambda b, pt: (b, 0, 0)
    return pl.pallas_call(
        _sscore_kernel,
        grid_spec=pltpu.PrefetchScalarGridSpec(
            num_scalar_prefetch=1,
            grid=(db,),
            in_specs=[pl.BlockSpec((1, IDX_HEADS, IDX_DIM), per_sample),
                      pl.BlockSpec((1, IDX_HEADS, 1), per_sample)]
                     + _page_specs(layer, n_pages, IDX_DIM, page),
            out_specs=pl.BlockSpec((1, 1, n_pages * page), per_sample)),
        out_shape=jax.ShapeDtypeStruct((db, 1, n_pages * page), F32),
        compiler_params=_cparams(1),
        name="sample_scores",
    )(page_table_flat, qi8, wi8, *([idx_t] * n_pages))


def _sselect_kernel(sc_ref, qi_ref, kin_ref, wi_ref, o_ref, keys_ref, hi_ref, lo_ref, tie_ref, *, top_k, past):
    n_chunks, page, db = keys_ref.shape
    n_pages = n_chunks - 1
    for p in range(n_pages):
        _store_keys(keys_ref, hi_ref, lo_ref, p, sc_ref[:, p * page:(p + 1) * page].T)
    kin = kin_ref[...].astype(BF16).astype(F32)
    qif = qi_ref[...].astype(F32)
    w = wi_ref[...]
    s_new = jnp.zeros((db, 1), F32)
    for h in range(IDX_HEADS):
        qh = qif[:, h * IDX_DIM:(h + 1) * IDX_DIM]
        s_new = s_new + w[:, h:h + 1] * jnp.maximum(jnp.sum(qh * kin, axis=-1, keepdims=True), 0.0)
    lane = lax.broadcasted_iota(I32, (db, page), 1)
    _store_keys(keys_ref, hi_ref, lo_ref, n_pages, jnp.where(lane == 0, s_new, NEG_INF).T)
    _park_chunk(hi_ref, lo_ref, n_chunks)

    thr, _ = _select_threshold(keys_ref, hi_ref, lo_ref, n_chunks, top_k, tie_ref)
    tie = tie_ref[...]
    row = lax.broadcasted_iota(I32, (page, db), 0)
    for p in range(n_chunks):
        key_index = p * page + row
        sel = jnp.logical_and(key_index <= past, _selected(keys_ref[p], key_index, thr, tie))
        o_ref[:, p * page:(p + 1) * page] = jnp.where(sel, 0.0, NEG_INF).T


def _sample_select(scores, qi, ki_new, wi, top_k, past, page):
    db = scores.shape[0]
    n_chunks = past // page + 1
    return pl.pallas_call(
        functools.partial(_sselect_kernel, top_k=top_k, past=past),
        out_shape=jax.ShapeDtypeStruct((db, n_chunks * page), F32),
        scratch_shapes=[pltpu.VMEM((n_chunks, page, db), I32),
                        pltpu.VMEM((n_chunks + 1, page, db), I16),
                        pltpu.VMEM((n_chunks + 1, page, db), I16),
                        pltpu.VMEM((1, db), I32)],
        compiler_params=pltpu.CompilerParams(vmem_limit_bytes=VMEM_LIMIT_BYTES),
        name="sample_select",
    )(scores, qi, ki_new, wi)


def _sattn_kernel(pt_ref, q_ref, mask_ref, bias_ref, kn_ref, vn_ref, *refs):
    n_pages = (len(refs) - 1) // 2
    k_refs, v_refs, o_ref = refs[:n_pages], refs[n_pages:2 * n_pages], refs[-1]
    a = q_ref.shape[2]
    page = k_refs[0].shape[3]
    q = q_ref[0]
    lane_head = lax.shift_right_logical(lax.broadcasted_iota(I32, (N_HEADS, a), 1),
                                        HEAD_DIM.bit_length() - 1)
    own = lane_head == lax.broadcasted_iota(I32, (N_HEADS, a), 0)
    qf = q.astype(F32)
    q_heads = jnp.where(own, qf, 0.0).astype(BF16)
    parts = [_dot(q_heads, k_ref[0, 0].astype(BF16)) for k_ref in k_refs]
    kn = kn_ref[0].astype(BF16).astype(F32)
    g_new = jnp.sum(jnp.where(own, qf * kn, 0.0), axis=-1, keepdims=True)
    parts.append(jnp.where(lax.broadcasted_iota(I32, (N_HEADS, page), 1) == 0, g_new, 0.0))
    s = jnp.concatenate(parts, axis=1) + bias_ref[...] + mask_ref[0]
    p = jnp.exp2(s - jnp.max(s, axis=-1, keepdims=True))
    l = jnp.sum(p, axis=-1, keepdims=True)
    new_col = n_pages * page
    acc = p[:, new_col:new_col + 1] * vn_ref[0].astype(BF16).astype(F32)
    for pg, v_ref in enumerate(v_refs):
        acc = acc + _dot_nt(p[:, pg * page:(pg + 1) * page].astype(BF16), v_ref[0, 0].astype(BF16))
    o_ref[0] = jnp.sum(jnp.where(own, acc / l, 0.0), axis=0, keepdims=True).astype(BF16)


def _sample_attention(layer, page_table_flat, q3, mask3, bias_s, k_new3, v_new3, kt_cache, vt_cache, n_pages):
    db, _, a = q3.shape
    page = kt_cache.shape[3]
    n_keys = mask3.shape[2]
    per_sample = lambda b, pt: (b, 0, 0)
    return pl.pallas_call(
        _sattn_kernel,
        grid_spec=pltpu.PrefetchScalarGridSpec(
            num_scalar_prefetch=1,
            grid=(db,),
            in_specs=[pl.BlockSpec((1, 1, a), per_sample),
                      pl.BlockSpec((1, 1, n_keys), per_sample),
                      _resident((N_HEADS, n_keys), lambda b, pt: (0, 0)),
                      pl.BlockSpec((1, 1, a), per_sample),
                      pl.BlockSpec((1, 1, a), per_sample)]
                     + _page_specs(layer, n_pages, a, page) + _page_specs(layer, n_pages, a, page),
            out_specs=pl.BlockSpec((1, 1, a), per_sample)),
        out_shape=jax.ShapeDtypeStruct((db, 1, a), BF16),
        compiler_params=_cparams(1),
        name="sample_attn",
    )(page_table_flat, q3, mask3, bias_s, k_new3, v_new3, *([kt_cache] * n_pages), *([vt_cache] * n_pages))


def _merge_kernel(h_ref, z_ref, zp1_ref, zp2_ref, prev_ref, bc_ref, sgc_ref, sga_ref, o_ref, cw_ref,
                  woc_ref, woa_ref, wout_ref, x_ref, *, tiles_per_seq, sequence):
    z = z_ref[...]
    if sequence:
        first = (pl.program_id(0) % tiles_per_seq) == 0
        before1 = jnp.where(first, prev_ref[0, 1:2, :], zp1_ref[7:8, :])
        before2 = jnp.where(first, prev_ref[0, 0:1, :], zp1_ref[6:7, :])
        rowi = lax.broadcasted_iota(I32, z.shape, 0)
        z1 = jnp.where(rowi == 0, before1, pltpu.roll(z, 1, 0))
        z2 = jnp.where(rowi == 0, before2, jnp.where(rowi == 1, before1, pltpu.roll(z, 2, 0)))
    else:
        z1 = zp1_ref[...]
        z2 = zp2_ref[...]
    conv = cw_ref[0:1, :] * z2 + cw_ref[1:2, :] * z1 + cw_ref[2:3, :] * z
    y_conv = _dot((bc_ref[...] * conv).astype(BF16), woc_ref[...])
    y_attn = _dot(o_ref[...], woa_ref[...])
    merged = sgc_ref[...] * y_conv + sga_ref[...] * y_attn
    x_ref[...] = h_ref[...] + _dot(merged.astype(BF16), wout_ref[...])


def _merge(h, z, zp1, zp2, prev, bc, sgc, sga, o, cw, woc, woa, wout, batch, sequence):
    m, d = h.shape
    c = z.shape[1]
    seq = m // batch
    tm = min(ROW_TILE, seq) if sequence else min(ROW_TILE, m)
    tiles_per_seq = seq // tm if sequence else 1
    row = lambda i: (i, 0)
    const = lambda i: (0, 0)
    if sequence:
        halo = tm // SUBLANES
        zp1_spec = pl.BlockSpec((SUBLANES, c), lambda i: (jnp.maximum(i * halo - 1, 0), 0))
        zp2_spec = pl.BlockSpec((SUBLANES, c), lambda i: (0, 0))
        prev_spec = pl.BlockSpec((1, CONV_W - 1, c), lambda i: (i // tiles_per_seq, 0, 0))
    else:
        zp1_spec = pl.BlockSpec((tm, c), row)
        zp2_spec = pl.BlockSpec((tm, c), row)
        prev_spec = pl.BlockSpec((1, CONV_W - 1, c), lambda i: (0, 0, 0))
    return pl.pallas_call(
        functools.partial(_merge_kernel, tiles_per_seq=tiles_per_seq, sequence=sequence),
        grid=(m // tm,),
        in_specs=[pl.BlockSpec((tm, d), row),
                  pl.BlockSpec((tm, c), row),
                  zp1_spec, zp2_spec, prev_spec,
                  pl.BlockSpec((tm, c), row),
                  pl.BlockSpec((tm, d), row),
                  pl.BlockSpec((tm, d), row),
                  pl.BlockSpec((tm, o.shape[1]), row),
                  _resident((CONV_W, c), const),
                  _resident(woc.shape, const),
                  _resident(woa.shape, const),
                  _resident(wout.shape, const)],
        out_specs=pl.BlockSpec((tm, d), row),
        out_shape=jax.ShapeDtypeStruct((m, d), F32),
        compiler_params=_cparams(1),
        name="merge",
    )(h, z, zp1, zp2, prev, bc, sgc, sga, o, cw, woc, woa, wout)


def _arrange_w_in(w, d):
    c = a = d // 2
    o = 3 * c + 3 * a + IDX_HEADS * IDX_DIM
    head = w[:, :o]
    ki = jnp.pad(w[:, o:o + IDX_DIM], ((0, 0), (0, LANES - IDX_DIM)))
    wi = jnp.pad(w[:, o + IDX_DIM:o + IDX_DIM + IDX_HEADS], ((0, 0), (0, LANES - IDX_HEADS)))
    gates = w[:, o + IDX_DIM + IDX_HEADS:]
    return jnp.concatenate([head, ki, wi, gates], axis=1).astype(BF16)


def kernel(x_prompt, x_sample, cache_k, cache_v, cache_idx_k, state_conv, page_table, rel_bias,
           norm_ffn1, w_ffn1_in, w_ffn1_out, norm_mix, w_in, conv_w, w_o_conv, w_o_attn, w_out,
           norm_ffn2, w_ffn2_in, w_ffn2_out, norm_final):
    batch, seq, d = x_prompt.shape
    db, dec_seq, _ = x_sample.shape
    assert dec_seq == 1
    depth = w_in.shape[0]
    c = a = d // 2
    n_pool, page = cache_k.shape[1:3]
    n_pages = page_table.shape[1]
    past = n_pages * page
    n_keys_s = past + page
    t = min(ATT_TILE, seq)
    assert seq % t == 0 and seq % min(ROW_TILE, seq) == 0 and t >= REL_MAX_DIST
    top_k_p = min(TOPK_MAX, seq // 4)
    top_k_s = min(TOPK_MAX, (past + dec_seq) // 4)

    bias0, bias1, bias_s = _bias_tables(rel_bias, t, past, n_keys_s)
    pt_flat = page_table.reshape(-1)
    g_final = norm_final.reshape(1, d)
    conv_zero = jnp.zeros((batch, CONV_W - 1, c), F32)
    kt_cache = jnp.transpose(cache_k, (0, 1, 3, 4, 2)).reshape(depth, n_pool, a, page)
    vt_cache = jnp.transpose(cache_v, (0, 1, 3, 4, 2)).reshape(depth, n_pool, a, page)
    idx_t = jnp.transpose(cache_idx_k, (0, 1, 3, 2))

    xp = x_prompt.reshape(batch * seq, d)
    xs = x_sample.reshape(db, d)
    outs_p = {"k": [], "v": [], "ik": [], "conv": []}
    outs_s = {"k": [], "v": [], "ik": [], "conv": []}

    for l in range(depth):
        last = l == depth - 1
        g1 = norm_ffn1[l].reshape(1, d)
        g2 = norm_ffn2[l].reshape(1, d)
        gm = norm_mix[l].reshape(1, d)
        w1_in, w1_out = w_ffn1_in[l].astype(BF16), w_ffn1_out[l].astype(BF16)
        w2_in, w2_out = w_ffn2_in[l].astype(BF16), w_ffn2_out[l].astype(BF16)
        wp = _arrange_w_in(w_in[l], d)
        woc, woa, wout = w_o_conv[l].astype(BF16), w_o_attn[l].astype(BF16), w_out[l].astype(BF16)
        cw = conv_w[l]

        hp = _ffn(xp, g1, w1_in, w1_out, g_final, False)
        (z, bc, sgc, sga, kt, vt, kit, kb, ki2, vtb, qt, qit, wit) = _proj(hp, gm, wp, batch, True, t)
        o = _prompt_attention(qt, qit, wit, kb, vtb, ki2, bias0, bias1, top_k_p)
        x2 = _merge(hp, z, z, z, conv_zero, bc, sgc, sga, o, cw, woc, woa, wout, batch, True)
        xp = _ffn(x2, g2, w2_in, w2_out, g_final, last)
        outs_p["k"].append(kt)
        outs_p["v"].append(vt)
        outs_p["ik"].append(kit)
        outs_p["conv"].append(z.reshape(batch, seq, c)[:, seq - (CONV_W - 1):])

        hs = _ffn(xs, g1, w1_in, w1_out, g_final, False)
        (z, bc, sgc, sga, kt, vt, kit, q, qi, wi, k, v, ki) = _proj(hs, gm, wp, 1, False)
        scores = _sample_scores(l, pt_flat, qi.reshape(db, IDX_HEADS, IDX_DIM),
                                wi.reshape(db, IDX_HEADS, 1), idx_t, n_pages)
        mask = _sample_select(scores.reshape(db, past), qi, ki, wi, top_k_s, past, page)
        o = _sample_attention(l, pt_flat, q.reshape(db, 1, a), mask.reshape(db, 1, n_keys_s), bias_s,
                              k.reshape(db, 1, a), v.reshape(db, 1, a), kt_cache, vt_cache, n_pages)
        prev = state_conv[l]
        x2 = _merge(hs, z, prev[:, 1], prev[:, 0], conv_zero, bc, sgc, sga, o.reshape(db, a), cw,
                    woc, woa, wout, 1, False)
        xs = _ffn(x2, g2, w2_in, w2_out, g_final, last)
        outs_s["k"].append(kt)
        outs_s["v"].append(vt)
        outs_s["ik"].append(kit)
        outs_s["conv"].append(jnp.stack([prev[:, 1], z], axis=1))

    def heads_last(parts, lead, n_pos):
        x = jnp.stack(parts).reshape(depth, -1, N_HEADS, HEAD_DIM, n_pos)
        return jnp.transpose(x, (0, 1, 4, 2, 3)).reshape((depth,) + lead + (N_HEADS, HEAD_DIM))

    def dim_last(parts, lead, n_pos):
        x = jnp.transpose(jnp.stack(parts), (0, 1, 3, 2))
        return x.reshape((depth,) + lead + (IDX_DIM,))

    return (xp.reshape(batch, seq, d),
            xs.reshape(db, dec_seq, d),
            heads_last(outs_p["k"], (batch, seq), seq),
            heads_last(outs_p["v"], (batch, seq), seq),
            dim_last(outs_p["ik"], (batch, seq), seq),
            jnp.stack(outs_p["conv"]),
            heads_last(outs_s["k"], (db, dec_seq), db),
            heads_last(outs_s["v"], (db, dec_seq), db),
            dim_last(outs_s["ik"], (db, dec_seq), db),
            jnp.stack(outs_s["conv"]))
```

```python
import functools
import math

import jax
import jax.numpy as jnp
from jax import lax
from jax.experimental import pallas as pl
from jax.experimental.pallas import tpu as pltpu

F32 = jnp.float32
BF16 = jnp.bfloat16
I32 = jnp.int32

EPS = 1e-6
NEG_INF = -1e30
HEAD_DIM = 64
N_HEADS = 8
IDX_HEADS = 8
IDX_DIM = 64
TOPK_MAX = 256
N_BUCKETS = 32
REL_MAX_DIST = 128
CONV_W = 3
LANES = 128
SUBLANES = 8
HEADS_PER_VREG = LANES // HEAD_DIM
VMEM_LIMIT_BYTES = 56 * 1024 * 1024
ROW_TILE = 512
ATT_TILE = 256
FFN_CHUNK = 256
SCORE_SAMPLES_PER_STEP = 4
INT_MAX = 2**31 - 1
LOG2E = math.log2(math.e)
I16 = jnp.int16
PACKED_ROWS = 2 * SUBLANES
HALF_BITS = 16
HALF_MASK = 2**HALF_BITS - 1
HALF_BIAS = 2**(HALF_BITS - 1)


def _cparams(n_axes):
    return pltpu.CompilerParams(dimension_semantics=("arbitrary",) * n_axes,
                                vmem_limit_bytes=VMEM_LIMIT_BYTES)


def _resident(shape, index_map):
    return pl.BlockSpec(shape, index_map, pipeline_mode=pl.Buffered(1))


def _rms(x, g):
    return x * lax.rsqrt(jnp.mean(x * x, axis=-1, keepdims=True) + EPS) * g


def _sigmoid(x):
    return 1.0 / (1.0 + jnp.exp(-x))


def _dot(a, b):
    return jnp.dot(a, b, preferred_element_type=F32)


def _dot_nt(a, b):
    return lax.dot_general(a, b, (((1,), (1,)), ((), ())), preferred_element_type=F32)


def _ffn_kernel(x_ref, g_ref, wgu_ref, wd_ref, gf_ref, o_ref, acc_ref, *, d_ff, final_norm):
    x = x_ref[...]
    xb = _rms(x, g_ref[...]).astype(BF16)
    for j in range(d_ff // FFN_CHUNK):
        lo = j * FFN_CHUNK
        a = _dot(xb, wgu_ref[:, lo:lo + FFN_CHUNK])
        b = _dot(xb, wgu_ref[:, d_ff + lo:d_ff + lo + FFN_CHUNK])
        hid = (a * _sigmoid(a) * b).astype(BF16)
        part = _dot(hid, wd_ref[lo:lo + FFN_CHUNK, :])
        if j == 0:
            acc_ref[...] = part
        else:
            acc_ref[...] += part
    y = x + 0.5 * acc_ref[...]
    if final_norm:
        y = _rms(y, gf_ref[...])
    o_ref[...] = y


def _ffn(x, g, wgu, wd, g_final, final_norm):
    m, d = x.shape
    d_ff = wd.shape[0]
    tm = min(ROW_TILE, m)
    return pl.pallas_call(
        functools.partial(_ffn_kernel, d_ff=d_ff, final_norm=final_norm),
        grid=(m // tm,),
        in_specs=[pl.BlockSpec((tm, d), lambda i: (i, 0)),
                  _resident((1, d), lambda i: (0, 0)),
                  _resident((d, 2 * d_ff), lambda i: (0, 0)),
                  _resident((d_ff, d), lambda i: (0, 0)),
                  _resident((1, d), lambda i: (0, 0))],
        out_specs=pl.BlockSpec((tm, d), lambda i: (i, 0)),
        out_shape=jax.ShapeDtypeStruct((m, d), F32),
        scratch_shapes=[pltpu.VMEM((tm, d), F32)],
        compiler_params=_cparams(1),
        name="ffn",
    )(x, g, wgu, wd, g_final)


def _proj_kernel(h_ref, g_ref, w_ref, z_ref, bc_ref, sgc_ref, sga_ref, kt_ref, vt_ref, kit_ref, *refs,
                 c, a, d, prompt):
    u = _rms(h_ref[...], g_ref[...]).astype(BF16)

    def mm(lo, hi):
        return _dot(u, w_ref[:, lo:hi])

    z_ref[...] = mm(2 * c, 3 * c) * mm(0, c)
    bc_ref[...] = mm(c, 2 * c)
    o = 3 * c
    q = mm(o, o + a)
    k = mm(o + a, o + 2 * a)
    v = mm(o + 2 * a, o + 3 * a)
    o += 3 * a
    qi = mm(o, o + IDX_HEADS * IDX_DIM) * IDX_DIM ** -0.5
    o += IDX_HEADS * IDX_DIM
    ki = mm(o, o + LANES)
    wi = mm(o + LANES, o + 2 * LANES) * IDX_HEADS ** -0.5
    o += 2 * LANES
    sgc_ref[...] = _sigmoid(mm(o, o + d))
    sga_ref[...] = _sigmoid(mm(o + d, o + 2 * d))
    kt = k.T
    vt = v.T
    kt_ref[0] = kt
    vt_ref[0] = vt
    kit_ref[0] = ki.T[:IDX_DIM, :]
    if prompt:
        kb_ref, ki2_ref, vtb_ref, qt_ref, qit_ref, wit_ref = refs
        kb_ref[...] = k.astype(BF16)
        ki2_ref[...] = (ki + pltpu.roll(ki, IDX_DIM, 1)).astype(BF16)
        _, n_sub, n_pairs, v_rows, t = vtb_ref.shape
        for j in range(n_sub):
            for pr in range(n_pairs):
                vtb_ref[0, j, pr, :LANES, :] = vt[pr * LANES:(pr + 1) * LANES, j * t:(j + 1) * t].astype(BF16)
                vtb_ref[0, j, pr, LANES:, :] = jnp.ones((v_rows - LANES, t), BF16)
        qt_ref[0] = (q * (HEAD_DIM ** -0.5 * LOG2E)).T.astype(BF16)
        qit_ref[0] = qi.T.astype(BF16)
        wit_ref[0] = wi.T[:IDX_HEADS, :]
    else:
        q_ref, qi_ref, wi_ref, k_ref, v_ref, ki_ref = refs
        q_ref[...] = (q * (HEAD_DIM ** -0.5 * LOG2E)).astype(BF16)
        qi_ref[...] = qi.astype(BF16)
        wi_ref[...] = wi[:, :IDX_HEADS]
        k_ref[...] = k
        v_ref[...] = v
        ki_ref[...] = ki[:, :IDX_DIM]


def _proj(h, g, w, batch, prompt, att_tile=None):
    m, d = h.shape
    n_cols = w.shape[1]
    c = a = d // 2
    seq = m // batch
    tm = min(ROW_TILE, seq)
    tiles_per_seq = seq // tm
    row = lambda i: (i, 0)
    fmaj = lambda i: (i // tiles_per_seq, 0, i % tiles_per_seq)
    chunked = lambda i: (i // tiles_per_seq, i % tiles_per_seq, 0, 0)

    def nat(cols, dtype):
        return jax.ShapeDtypeStruct((m, cols), dtype), pl.BlockSpec((tm, cols), row)

    def fm(rows, dtype):
        return jax.ShapeDtypeStruct((batch, rows, seq), dtype), pl.BlockSpec((1, rows, tm), fmaj)

    outs = [nat(c, F32),
            nat(c, F32),
            nat(d, F32),
            nat(d, F32),
            fm(a, F32),
            fm(a, F32),
            fm(IDX_DIM, F32)]
    if prompt:
        assert tm % att_tile == 0
        outs += [nat(a, BF16),
                 nat(LANES, BF16),
                 (jax.ShapeDtypeStruct((batch, seq // att_tile, a // LANES, LANES + PACKED_ROWS, att_tile), BF16),
                  pl.BlockSpec((1, tm // att_tile, a // LANES, LANES + PACKED_ROWS, att_tile),
                               lambda i: (i // tiles_per_seq, i % tiles_per_seq, 0, 0, 0))),
                 fm(a, BF16),
                 fm(a, BF16),
                 fm(IDX_HEADS, F32)]
    else:
        outs += [nat(a, BF16), nat(a, BF16), nat(IDX_HEADS, F32), nat(a, F32), nat(a, F32), nat(IDX_DIM, F32)]
    return pl.pallas_call(
        functools.partial(_proj_kernel, c=c, a=a, d=d, prompt=prompt),
        grid=(m // tm,),
        in_specs=[pl.BlockSpec((tm, d), row),
                  _resident((1, d), lambda i: (0, 0)),
                  _resident((d, n_cols), lambda i: (0, 0))],
        out_specs=[o[1] for o in outs],
        out_shape=[o[0] for o in outs],
        compiler_params=_cparams(1),
        name="proj",
    )(h, g, w)


def _sortable(x):
    x = jnp.where(x == 0.0, 0.0, x)
    b = lax.bitcast_convert_type(x, I32)
    return b ^ ((b >> 31) & I32(INT_MAX))


def _store_keys(keys_ref, hi_ref, lo_ref, ci, scores):
    k = _sortable(scores)
    keys_ref[ci] = k
    hi_ref[ci] = (k >> HALF_BITS).astype(I16)
    lo_ref[ci] = ((k & HALF_MASK) - HALF_BIAS).astype(I16)


def _count(keys_ref, n_chunks, pred):
    kc, nq = keys_ref.shape[1:]

    def body(ci, acc):
        hit = jnp.where(pred(keys_ref[ci], ci), 1.0, 0.0)
        return acc + jnp.sum(hit.reshape(kc // SUBLANES, SUBLANES, nq), axis=0)

    acc = lax.fori_loop(0, n_chunks, body, jnp.zeros((SUBLANES, nq), F32))
    return jnp.sum(acc, axis=0, keepdims=True)


def _count16(x_ref, n_chunks, bound, strict=False):
    kc, nq = x_ref.shape[1:]
    b16 = bound.astype(I16)

    def body(j, acc):
        for ci in (2 * j, 2 * j + 1):
            x = x_ref[ci]
            hit = jnp.where(x > b16 if strict else x >= b16, jnp.int16(1), jnp.int16(0))
            for r in range(kc // PACKED_ROWS):
                acc = acc + hit[r * PACKED_ROWS:(r + 1) * PACKED_ROWS, :]
        return acc

    acc = lax.fori_loop(0, (n_chunks + 1) // 2, body, jnp.zeros((PACKED_ROWS, nq), I16))
    return jnp.sum(acc.astype(I32), axis=0, keepdims=True).astype(F32)


def _park_chunk(hi_ref, lo_ref, ci):
    hi_ref[ci] = jnp.full(hi_ref.shape[1:], -HALF_BIAS, I16)
    lo_ref[ci] = jnp.full(lo_ref.shape[1:], -HALF_BIAS, I16)


def _radix16(x_ref, n_chunks, rank):
    nq = x_ref.shape[2]

    def bit_step(it, prefix):
        cand = prefix | lax.shift_left(I32(1), HALF_BITS - 1 - it)
        cnt = _count16(x_ref, n_chunks, cand - HALF_BIAS)
        return jnp.where(cnt >= rank, cand, prefix)

    return lax.fori_loop(0, HALF_BITS, bit_step, jnp.zeros((1, nq), I32))


def _select_threshold(keys_ref, hi_ref, lo_ref, n_chunks, top_k, tie_ref):
    _, kc, nq = keys_ref.shape
    kf = float(top_k)

    hi = _radix16(hi_ref, n_chunks, kf) - HALF_BIAS
    above = _count16(hi_ref, n_chunks, hi, strict=True)
    hi16 = hi.astype(I16)

    def keep_equal_high(ci, carry):
        lo_ref[ci] = jnp.where(hi_ref[ci] == hi16, lo_ref[ci], jnp.int16(-HALF_BIAS))
        return carry

    lax.fori_loop(0, n_chunks, keep_equal_high, 0)
    lo_bits = _radix16(lo_ref, n_chunks, kf - above)
    thr = jnp.left_shift(hi, HALF_BITS) | lo_bits
    cnt_gt = above + _count16(lo_ref, n_chunks, lo_bits - HALF_BIAS, strict=True)
    cnt_ge = above + jnp.where(lo_bits == 0, _count16(hi_ref, n_chunks, hi) - above,
                               _count16(lo_ref, n_chunks, lo_bits - HALF_BIAS))
    tie_ref[...] = jnp.full((1, nq), INT_MAX, I32)

    excess_ties = jnp.max(cnt_ge) > kf

    @pl.when(excess_ties)
    def _():
        need = kf - cnt_gt
        row = lax.broadcasted_iota(I32, (kc, nq), 0)
        n_bits = max(1, (keys_ref.shape[0] * kc - 1).bit_length())

        def idx_step(it, j):
            cand = j | lax.shift_left(I32(1), n_bits - 1 - it)
            cnt = _count(keys_ref, n_chunks,
                         lambda kk, ci: jnp.where(kk == thr, ci * kc + row, INT_MAX) < cand)
            return jnp.where(cnt < need, cand, j)

        tie_ref[...] = lax.fori_loop(0, n_bits, idx_step, jnp.zeros((1, nq), I32))

    return thr, excess_ties


def _selected(kk, key_index, thr, tie):
    return jnp.logical_or(kk > thr, jnp.logical_and(kk == thr, key_index <= tie))


def _head_rows(x_ref, h):
    rows = x_ref[0, h * HEAD_DIM:(h + 1) * HEAD_DIM, :]
    parts = [jnp.zeros_like(rows)] * HEADS_PER_VREG
    parts[h % HEADS_PER_VREG] = rows
    return jnp.concatenate(parts, axis=0)


def _attn_kernel(qt_ref, qit_ref, wit_ref, k_ref, vt_ref, ki2_ref, bias0_ref, bias1_ref, o_ref,
                 keys_ref, hi_ref, lo_ref, qm_ref, qim_ref, m_ref, alpha_ref, acc_ref, s_ref, p_ref,
                 tie_ref, *, top_k):
    t = qt_ref.shape[2]
    i = pl.program_id(1)
    key_row = lax.broadcasted_iota(I32, (t, t), 0)
    query_col = lax.broadcasted_iota(I32, (t, t), 1)
    causal = key_row <= query_col

    for h in range(N_HEADS):
        qm_ref[h] = _head_rows(qt_ref, h)
        qim_ref[h] = _head_rows(qit_ref, h)
    w = wit_ref[0]

    def score_chunk(ci, diagonal):
        kic = ki2_ref[0, ci]
        sc = jnp.zeros((t, t), F32)
        for h in range(IDX_HEADS):
            sc = sc + w[h:h + 1, :] * jnp.maximum(_dot(kic, qim_ref[h]), 0.0)
        if diagonal:
            sc = jnp.where(causal, sc, NEG_INF)
        _store_keys(keys_ref, hi_ref, lo_ref, ci, sc)

    def score_body(ci, carry):
        score_chunk(ci, False)
        return carry

    lax.fori_loop(0, i, score_body, 0)
    score_chunk(i, True)
    _park_chunk(hi_ref, lo_ref, i + 1)

    thr, _ = _select_threshold(keys_ref, hi_ref, lo_ref, i + 1, top_k, tie_ref)
    tie = tie_ref[...]

    def mask_chunk(ci, diagonal):
        sel = _selected(keys_ref[ci], ci * t + key_row, thr, tie)
        if diagonal:
            sel = jnp.logical_and(sel, causal)
        keys_ref[ci] = lax.bitcast_convert_type(jnp.where(sel, 0.0, NEG_INF), I32)

    def mask_body(ci, carry):
        mask_chunk(ci, False)
        return carry

    lax.fori_loop(0, i, mask_body, 0)
    mask_chunk(i, True)

    m_ref[...] = jnp.full(m_ref.shape, NEG_INF, F32)
    acc_ref[...] = jnp.zeros(acc_ref.shape, F32)

    def attend(units):
        def logits(h):
            pair = slice((h // HEADS_PER_VREG) * LANES, (h // HEADS_PER_VREG + 1) * LANES)
            top = None
            for u, (ci, mode) in enumerate(units):
                add = lax.bitcast_convert_type(keys_ref[ci], F32)
                if mode == "prev":
                    add = add + bias1_ref[h]
                elif mode == "diag":
                    add = add + bias0_ref[h]
                s = _dot(k_ref[0, ci, :, pair], qm_ref[h]) + add
                s_ref[u, h] = s
                s_max = jnp.max(s, axis=0, keepdims=True)
                top = s_max if top is None else jnp.maximum(top, s_max)
            m_old = m_ref[h]
            m_new = jnp.maximum(m_old, top)
            alpha_ref[h] = jnp.exp2(m_old - m_new)
            m_ref[h] = m_new

        def weights(h):
            for u in range(len(units)):
                p_ref[u, h] = jnp.exp2((s_ref[u, h] - m_ref[h]).astype(BF16))

        def accumulate(h):
            acc = alpha_ref[h] * acc_ref[h]
            for u, (ci, _) in enumerate(units):
                acc = acc + _dot(vt_ref[0, ci, h // HEADS_PER_VREG], p_ref[u, h])
            acc_ref[h] = acc

        for stage in (logits, weights, accumulate):
            for h in range(N_HEADS):
                stage(h)

    n_far = jnp.maximum(i - 1, 0)

    def far_body(j, carry):
        attend(((2 * j, "far"), (2 * j + 1, "far")))
        return carry

    lax.fori_loop(0, n_far // 2, far_body, 0)

    @pl.when(n_far % 2 == 1)
    def _():
        attend(((n_far - 1, "far"),))

    @pl.when(i >= 1)
    def _():
        attend(((i - 1, "prev"), (i, "diag")))

    @pl.when(i == 0)
    def _():
        attend(((i, "diag"),))

    for pr in range(N_HEADS // HEADS_PER_VREG):
        parts = []
        for hh in range(HEADS_PER_VREG):
            h = pr * HEADS_PER_VREG + hh
            parts.append(acc_ref[h, hh * HEAD_DIM:(hh + 1) * HEAD_DIM, :] / acc_ref[h, LANES:LANES + 1, :])
        o_ref[:, pr * LANES:(pr + 1) * LANES] = jnp.concatenate(parts, axis=0).T.astype(BF16)


def _prompt_attention(qt, qit, wit, kb, vtb5, ki2, bias0, bias1, top_k):
    batch, n_chunks, n_pairs, v_rows, t = vtb5.shape
    a = n_pairs * LANES
    seq = n_chunks * t
    qmap = lambda b, i: (b, 0, i)
    bmap = lambda b, i: (b, 0, 0, 0)
    cmap = lambda b, i: (0, 0, 0)
    kb4 = kb.reshape(batch, n_chunks, t, a)
    ki24 = ki2.reshape(batch, n_chunks, t, LANES)
    return pl.pallas_call(
        functools.partial(_attn_kernel, top_k=top_k),
        grid=(batch, n_chunks),
        in_specs=[pl.BlockSpec((1, a, t), qmap),
                  pl.BlockSpec((1, a, t), qmap),
                  pl.BlockSpec((1, IDX_HEADS, t), qmap),
                  _resident((1, n_chunks, t, a), bmap),
                  _resident((1, n_chunks, n_pairs, v_rows, t), lambda b, i: (b, 0, 0, 0, 0)),
                  _resident((1, n_chunks, t, LANES), bmap),
                  _resident((N_HEADS, t, t), cmap),
                  _resident((N_HEADS, t, t), cmap)],
        out_specs=pl.BlockSpec((t, a), lambda b, i: (b * n_chunks + i, 0)),
        out_shape=jax.ShapeDtypeStruct((batch * seq, a), BF16),
        scratch_shapes=[pltpu.VMEM((n_chunks, t, t), I32),
                        pltpu.VMEM((n_chunks + 1, t, t), I16),
                        pltpu.VMEM((n_chunks + 1, t, t), I16),
                        pltpu.VMEM((N_HEADS, LANES, t), BF16),
                        pltpu.VMEM((IDX_HEADS, LANES, t), BF16),
                        pltpu.VMEM((N_HEADS, 1, t), F32),
                        pltpu.VMEM((N_HEADS, 1, t), F32),
                        pltpu.VMEM((N_HEADS, v_rows, t), F32),
                        pltpu.VMEM((2, N_HEADS, t, t), F32),
                        pltpu.VMEM((2, N_HEADS, t, t), BF16),
                        pltpu.VMEM((1, t), I32)],
        compiler_params=_cparams(2),
        name="prompt_attn",
    )(qt, qit, wit, kb4, vtb5, ki24, bias0, bias1)


def _bias_of_dist(dist, rb_ref, h):
    n = jnp.maximum(dist, 0)
    max_exact = N_BUCKETS // 2
    nf = jnp.maximum(n, 1).astype(F32)
    large = max_exact + (jnp.log(nf / max_exact) / math.log(REL_MAX_DIST / max_exact)
                         * (N_BUCKETS - max_exact)).astype(I32)
    large = jnp.minimum(large, N_BUCKETS - 1)
    bucket = jnp.where(n < max_exact, n, large)
    out = jnp.zeros(dist.shape, F32)
    for nb in range(N_BUCKETS):
        out = jnp.where(bucket == nb, rb_ref[nb, h], out)
    return (out - rb_ref[N_BUCKETS - 1, h]) * LOG2E


def _bias_kernel(rb_ref, b0_ref, b1_ref, bs_ref, *, t, past):
    key_row = lax.broadcasted_iota(I32, (t, t), 0)
    query_col = lax.broadcasted_iota(I32, (t, t), 1)
    key_lane = lax.broadcasted_iota(I32, (1, bs_ref.shape[1]), 1)
    for h in range(N_HEADS):
        b0_ref[h] = _bias_of_dist(query_col - key_row, rb_ref, h)
        b1_ref[h] = _bias_of_dist(t + query_col - key_row, rb_ref, h)
        bs_ref[h:h + 1, :] = _bias_of_dist(past - key_lane, rb_ref, h)


def _bias_tables(rel_bias, t, past, n_keys_s):
    return pl.pallas_call(
        functools.partial(_bias_kernel, t=t, past=past),
        in_specs=[pl.BlockSpec(memory_space=pltpu.SMEM)],
        out_shape=[jax.ShapeDtypeStruct((N_HEADS, t, t), F32),
                   jax.ShapeDtypeStruct((N_HEADS, t, t), F32),
                   jax.ShapeDtypeStruct((N_HEADS, n_keys_s), F32)],
        compiler_params=pltpu.CompilerParams(vmem_limit_bytes=VMEM_LIMIT_BYTES),
        name="bias_tables",
    )(rel_bias)


def _page_specs(layer, n_pages, rows, page, samples_per_step=1):
    def spec(g, p):
        return pl.BlockSpec((1, 1, rows, page),
                            lambda b, pt: (layer, pt[(b * samples_per_step + g) * n_pages + p], 0, 0))
    return [spec(g, p) for g in range(samples_per_step) for p in range(n_pages)]


def _sscore_kernel(pt_ref, qi_ref, wi_ref, *refs, n_pages):
    ik_refs, o_ref = refs[:-1], refs[-1]
    page = ik_refs[0].shape[3]
    for j, ik_ref in enumerate(ik_refs):
        g, p = divmod(j, n_pages)
        dots = _dot(qi_ref[g], ik_ref[0, 0].astype(BF16))
        o_ref[g, :, p * page:(p + 1) * page] = jnp.sum(wi_ref[g] * jnp.maximum(dots, 0.0),
                                                       axis=0, keepdims=True)


def _sample_scores(layer, page_table_flat, qi8, wi8, idx_t, n_pages):
    db = qi8.shape[0]
    page = idx_t.shape[3]
    g = math.gcd(db, SCORE_SAMPLES_PER_STEP)
    per_step = lambda b, pt: (b, 0, 0)
    return pl.pallas_call(
        functools.partial(_sscore_kernel, n_pages=n_pages),
        grid_spec=pltpu.PrefetchScalarGridSpec(
            num_scalar_prefetch=1,
            grid=(db // g,),
            in_specs=[pl.BlockSpec((g, IDX_HEADS, IDX_DIM), per_step),
                      pl.BlockSpec((g, IDX_HEADS, 1), per_step)]
                     + _page_specs(layer, n_pages, IDX_DIM, page, g),
            out_specs=pl.BlockSpec((g, 1, n_pages * page), per_step)),
        out_shape=jax.ShapeDtypeStruct((db, 1, n_pages * page), F32),
        compiler_params=_cparams(1),
        name="sample_scores",
    )(page_table_flat, qi8, wi8, *([idx_t] * (g * n_pages)))


def _sselect_kernel(sc_ref, qi_ref, kin_ref, wi_ref, o_ref, keys_ref, hi_ref, lo_ref, tie_ref, *, top_k, past):
    n_chunks, page, db = keys_ref.shape
    n_pages = n_chunks - 1
    for p in range(n_pages):
        _store_keys(keys_ref, hi_ref, lo_ref, p, sc_ref[:, p * page:(p + 1) * page].T)
    kin = kin_ref[...].astype(BF16).astype(F32)
    qif = qi_ref[...].astype(F32)
    w = wi_ref[...]
    s_new = jnp.zeros((db, 1), F32)
    for h in range(IDX_HEADS):
        qh = qif[:, h * IDX_DIM:(h + 1) * IDX_DIM]
        s_new = s_new + w[:, h:h + 1] * jnp.maximum(jnp.sum(qh * kin, axis=-1, keepdims=True), 0.0)
    lane = lax.broadcasted_iota(I32, (db, page), 1)
    _store_keys(keys_ref, hi_ref, lo_ref, n_pages, jnp.where(lane == 0, s_new, NEG_INF).T)
    _park_chunk(hi_ref, lo_ref, n_chunks)

    thr, _ = _select_threshold(keys_ref, hi_ref, lo_ref, n_chunks, top_k, tie_ref)
    tie = tie_ref[...]
    row = lax.broadcasted_iota(I32, (page, db), 0)
    for p in range(n_chunks):
        key_index = p * page + row
        sel = jnp.logical_and(key_index <= past, _selected(keys_ref[p], key_index, thr, tie))
        o_ref[:, p * page:(p + 1) * page] = jnp.where(sel, 0.0, NEG_INF).T


def _sample_select(scores, qi, ki_new, wi, top_k, past, page):
    db = scores.shape[0]
    n_chunks = past // page + 1
    return pl.pallas_call(
        functools.partial(_sselect_kernel, top_k=top_k, past=past),
        out_shape=jax.ShapeDtypeStruct((db, n_chunks * page), F32),
        scratch_shapes=[pltpu.VMEM((n_chunks, page, db), I32),
                        pltpu.VMEM((n_chunks + 1, page, db), I16),
                        pltpu.VMEM((n_chunks + 1, page, db), I16),
                        pltpu.VMEM((1, db), I32)],
        compiler_params=pltpu.CompilerParams(vmem_limit_bytes=VMEM_LIMIT_BYTES),
        name="sample_select",
    )(scores, qi, ki_new, wi)


def _sattn_kernel(pt_ref, q_ref, mask_ref, bias_ref, kn_ref, vn_ref, *refs):
    n_pages = (len(refs) - 1) // 2
    k_refs, v_refs, o_ref = refs[:n_pages], refs[n_pages:2 * n_pages], refs[-1]
    a = q_ref.shape[2]
    page = k_refs[0].shape[3]
    q = q_ref[0]
    lane_head = lax.shift_right_logical(lax.broadcasted_iota(I32, (N_HEADS, a), 1),
                                        HEAD_DIM.bit_length() - 1)
    own = lane_head == lax.broadcasted_iota(I32, (N_HEADS, a), 0)
    qf = q.astype(F32)
    q_heads = jnp.where(own, qf, 0.0).astype(BF16)
    parts = [_dot(q_heads, k_ref[0, 0].astype(BF16)) for k_ref in k_refs]
    kn = kn_ref[0].astype(BF16).astype(F32)
    g_new = jnp.sum(jnp.where(own, qf * kn, 0.0), axis=-1, keepdims=True)
    parts.append(jnp.where(lax.broadcasted_iota(I32, (N_HEADS, page), 1) == 0, g_new, 0.0))
    s = jnp.concatenate(parts, axis=1) + bias_ref[...] + mask_ref[0]
    p = jnp.exp2(s - jnp.max(s, axis=-1, keepdims=True))
    l = jnp.sum(p, axis=-1, keepdims=True)
    new_col = n_pages * page
    acc = p[:, new_col:new_col + 1] * vn_ref[0].astype(BF16).astype(F32)
    for pg, v_ref in enumerate(v_refs):
        acc = acc + _dot_nt(p[:, pg * page:(pg + 1) * page].astype(BF16), v_ref[0, 0].astype(BF16))
    o_ref[0] = jnp.sum(jnp.where(own, acc / l, 0.0), axis=0, keepdims=True).astype(BF16)


def _sample_attention(layer, page_table_flat, q3, mask3, bias_s, k_new3, v_new3, kt_cache, vt_cache, n_pages):
    db, _, a = q3.shape
    page = kt_cache.shape[3]
    n_keys = mask3.shape[2]
    per_sample = lambda b, pt: (b, 0, 0)
    return pl.pallas_call(
        _sattn_kernel,
        grid_spec=pltpu.PrefetchScalarGridSpec(
            num_scalar_prefetch=1,
            grid=(db,),
            in_specs=[pl.BlockSpec((1, 1, a), per_sample),
                      pl.BlockSpec((1, 1, n_keys), per_sample),
                      _resident((N_HEADS, n_keys), lambda b, pt: (0, 0)),
                      pl.BlockSpec((1, 1, a), per_sample),
                      pl.BlockSpec((1, 1, a), per_sample)]
                     + _page_specs(layer, n_pages, a, page) + _page_specs(layer, n_pages, a, page),
            out_specs=pl.BlockSpec((1, 1, a), per_sample)),
        out_shape=jax.ShapeDtypeStruct((db, 1, a), BF16),
        compiler_params=_cparams(1),
        name="sample_attn",
    )(page_table_flat, q3, mask3, bias_s, k_new3, v_new3, *([kt_cache] * n_pages), *([vt_cache] * n_pages))


def _merge_kernel(h_ref, z_ref, zp1_ref, zp2_ref, prev_ref, bc_ref, sgc_ref, sga_ref, o_ref, cw_ref,
                  woc_ref, woa_ref, wout_ref, x_ref, *, tiles_per_seq, sequence):
    z = z_ref[...]
    if sequence:
        first = (pl.program_id(0) % tiles_per_seq) == 0
        before1 = jnp.where(first, prev_ref[0, 1:2, :], zp1_ref[7:8, :])
        before2 = jnp.where(first, prev_ref[0, 0:1, :], zp1_ref[6:7, :])
        rowi = lax.broadcasted_iota(I32, z.shape, 0)
        z1 = jnp.where(rowi == 0, before1, pltpu.roll(z, 1, 0))
        z2 = jnp.where(rowi == 0, before2, jnp.where(rowi == 1, before1, pltpu.roll(z, 2, 0)))
    else:
        z1 = zp1_ref[...]
        z2 = zp2_ref[...]
    conv = cw_ref[0:1, :] * z2 + cw_ref[1:2, :] * z1 + cw_ref[2:3, :] * z
    y_conv = _dot((bc_ref[...] * conv).astype(BF16), woc_ref[...])
    y_attn = _dot(o_ref[...], woa_ref[...])
    merged = sgc_ref[...] * y_conv + sga_ref[...] * y_attn
    x_ref[...] = h_ref[...] + _dot(merged.astype(BF16), wout_ref[...])


def _merge(h, z, zp1, zp2, prev, bc, sgc, sga, o, cw, woc, woa, wout, batch, sequence):
    m, d = h.shape
    c = z.shape[1]
    seq = m // batch
    tm = min(ROW_TILE, seq) if sequence else min(ROW_TILE, m)
    tiles_per_seq = seq // tm if sequence else 1
    row = lambda i: (i, 0)
    const = lambda i: (0, 0)
    if sequence:
        halo = tm // SUBLANES
        zp1_spec = pl.BlockSpec((SUBLANES, c), lambda i: (jnp.maximum(i * halo - 1, 0), 0))
        zp2_spec = pl.BlockSpec((SUBLANES, c), lambda i: (0, 0))
        prev_spec = pl.BlockSpec((1, CONV_W - 1, c), lambda i: (i // tiles_per_seq, 0, 0))
    else:
        zp1_spec = pl.BlockSpec((tm, c), row)
        zp2_spec = pl.BlockSpec((tm, c), row)
        prev_spec = pl.BlockSpec((1, CONV_W - 1, c), lambda i: (0, 0, 0))
    return pl.pallas_call(
        functools.partial(_merge_kernel, tiles_per_seq=tiles_per_seq, sequence=sequence),
        grid=(m // tm,),
        in_specs=[pl.BlockSpec((tm, d), row),
                  pl.BlockSpec((tm, c), row),
                  zp1_spec, zp2_spec, prev_spec,
                  pl.BlockSpec((tm, c), row),
                  pl.BlockSpec((tm, d), row),
                  pl.BlockSpec((tm, d), row),
                  pl.BlockSpec((tm, o.shape[1]), row),
                  _resident((CONV_W, c), const),
                  _resident(woc.shape, const),
                  _resident(woa.shape, const),
                  _resident(wout.shape, const)],
        out_specs=pl.BlockSpec((tm, d), row),
        out_shape=jax.ShapeDtypeStruct((m, d), F32),
        compiler_params=_cparams(1),
        name="merge",
    )(h, z, zp1, zp2, prev, bc, sgc, sga, o, cw, woc, woa, wout)


def _arrange_w_in(w, d):
    c = a = d // 2
    o = 3 * c + 3 * a + IDX_HEADS * IDX_DIM
    head = w[:, :o]
    ki = jnp.pad(w[:, o:o + IDX_DIM], ((0, 0), (0, LANES - IDX_DIM)))
    wi = jnp.pad(w[:, o + IDX_DIM:o + IDX_DIM + IDX_HEADS], ((0, 0), (0, LANES - IDX_HEADS)))
    gates = w[:, o + IDX_DIM + IDX_HEADS:]
    return jnp.concatenate([head, ki, wi, gates], axis=1).astype(BF16)


def kernel(x_prompt, x_sample, cache_k, cache_v, cache_idx_k, state_conv, page_table, rel_bias,
           norm_ffn1, w_ffn1_in, w_ffn1_out, norm_mix, w_in, conv_w, w_o_conv, w_o_attn, w_out,
           norm_ffn2, w_ffn2_in, w_ffn2_out, norm_final):
    batch, seq, d = x_prompt.shape
    db, dec_seq, _ = x_sample.shape
    assert dec_seq == 1
    depth = w_in.shape[0]
    c = a = d // 2
    n_pool, page = cache_k.shape[1:3]
    n_pages = page_table.shape[1]
    past = n_pages * page
    n_keys_s = past + page
    t = min(ATT_TILE, seq)
    assert seq % t == 0 and seq % min(ROW_TILE, seq) == 0 and t >= REL_MAX_DIST
    top_k_p = min(TOPK_MAX, seq // 4)
    top_k_s = min(TOPK_MAX, (past + dec_seq) // 4)

    bias0, bias1, bias_s = _bias_tables(rel_bias, t, past, n_keys_s)
    pt_flat = page_table.reshape(-1)
    g_final = norm_final.reshape(1, d)
    conv_zero = jnp.zeros((batch, CONV_W - 1, c), F32)
    kt_cache = jnp.transpose(cache_k, (0, 1, 3, 4, 2)).reshape(depth, n_pool, a, page)
    vt_cache = jnp.transpose(cache_v, (0, 1, 3, 4, 2)).reshape(depth, n_pool, a, page)
    idx_t = jnp.transpose(cache_idx_k, (0, 1, 3, 2))

    xp = x_prompt.reshape(batch * seq, d)
    xs = x_sample.reshape(db, d)
    outs_p = {"k": [], "v": [], "ik": [], "conv": []}
    outs_s = {"k": [], "v": [], "ik": [], "conv": []}

    for l in range(depth):
        last = l == depth - 1
        g1 = norm_ffn1[l].reshape(1, d)
        g2 = norm_ffn2[l].reshape(1, d)
        gm = norm_mix[l].reshape(1, d)
        w1_in, w1_out = w_ffn1_in[l].astype(BF16), w_ffn1_out[l].astype(BF16)
        w2_in, w2_out = w_ffn2_in[l].astype(BF16), w_ffn2_out[l].astype(BF16)
        wp = _arrange_w_in(w_in[l], d)
        woc, woa, wout = w_o_conv[l].astype(BF16), w_o_attn[l].astype(BF16), w_out[l].astype(BF16)
        cw = conv_w[l]

        hp = _ffn(xp, g1, w1_in, w1_out, g_final, False)
        (z, bc, sgc, sga, kt, vt, kit, kb, ki2, vtb, qt, qit, wit) = _proj(hp, gm, wp, batch, True, t)
        o = _prompt_attention(qt, qit, wit, kb, vtb, ki2, bias0, bias1, top_k_p)
        x2 = _merge(hp, z, z, z, conv_zero, bc, sgc, sga, o, cw, woc, woa, wout, batch, True)
        xp = _ffn(x2, g2, w2_in, w2_out, g_final, last)
        outs_p["k"].append(kt)
        outs_p["v"].append(vt)
        outs_p["ik"].append(kit)
        outs_p["conv"].append(z.reshape(batch, seq, c)[:, seq - (CONV_W - 1):])

        hs = _ffn(xs, g1, w1_in, w1_out, g_final, False)
        (z, bc, sgc, sga, kt, vt, kit, q, qi, wi, k, v, ki) = _proj(hs, gm, wp, 1, False)
        scores = _sample_scores(l, pt_flat, qi.reshape(db, IDX_HEADS, IDX_DIM),
                                wi.reshape(db, IDX_HEADS, 1), idx_t, n_pages)
        mask = _sample_select(scores.reshape(db, past), qi, ki, wi, top_k_s, past, page)
        o = _sample_attention(l, pt_flat, q.reshape(db, 1, a), mask.reshape(db, 1, n_keys_s), bias_s,
                              k.reshape(db, 1, a), v.reshape(db, 1, a), kt_cache, vt_cache, n_pages)
        prev = state_conv[l]
        x2 = _merge(hs, z, prev[:, 1], prev[:, 0], conv_zero, bc, sgc, sga, o.reshape(db, a), cw,
                    woc, woa, wout, 1, False)
        xs = _ffn(x2, g2, w2_in, w2_out, g_final, last)
        outs_s["k"].append(kt)
        outs_s["v"].append(vt)
        outs_s["ik"].append(kit)
        outs_s["conv"].append(jnp.stack([prev[:, 1], z], axis=1))

    def heads_last(parts, lead, n_pos):
        x = jnp.stack(parts).reshape(depth, -1, N_HEADS, HEAD_DIM, n_pos)
        return jnp.transpose(x, (0, 1, 4, 2, 3)).reshape((depth,) + lead + (N_HEADS, HEAD_DIM))

    def dim_last(parts, lead, n_pos):
        x = jnp.transpose(jnp.stack(parts), (0, 1, 3, 2))
        return x.reshape((depth,) + lead + (IDX_DIM,))

    return (xp.reshape(batch, seq, d),
            xs.reshape(db, dec_seq, d),
            heads_last(outs_p["k"], (batch, seq), seq),
            heads_last(outs_p["v"], (batch, seq), seq),
            dim_last(outs_p["ik"], (batch, seq), seq),
            jnp.stack(outs_p["conv"]),
            heads_last(outs_s["k"], (db, dec_seq), db),
            heads_last(outs_s["v"], (db, dec_seq), db),
            dim_last(outs_s["ik"], (db, dec_seq), db),
            jnp.stack(outs_s["conv"]))
```

```python
import functools
import math

import jax
import jax.numpy as jnp
from jax import lax
from jax.experimental import pallas as pl
from jax.experimental.pallas import tpu as pltpu

F32 = jnp.float32
BF16 = jnp.bfloat16
I32 = jnp.int32

EPS = 1e-6
NEG_INF = -1e30
HEAD_DIM = 64
N_HEADS = 8
IDX_HEADS = 8
IDX_DIM = 64
TOPK_MAX = 256
N_BUCKETS = 32
REL_MAX_DIST = 128
CONV_W = 3
LANES = 128
SUBLANES = 8
HEADS_PER_VREG = LANES // HEAD_DIM
VMEM_LIMIT_BYTES = 56 * 1024 * 1024
ROW_TILE = 512
ATT_TILE = 256
FFN_CHUNK = 256
SCORE_SAMPLES_PER_STEP = 4
INT_MAX = 2**31 - 1
LOG2E = math.log2(math.e)
I16 = jnp.int16
PACKED_ROWS = 2 * SUBLANES
HALF_BITS = 16
HALF_MASK = 2**HALF_BITS - 1
HALF_BIAS = 2**(HALF_BITS - 1)


def _cparams(n_axes):
    return pltpu.CompilerParams(dimension_semantics=("arbitrary",) * n_axes,
                                vmem_limit_bytes=VMEM_LIMIT_BYTES)


def _resident(shape, index_map):
    return pl.BlockSpec(shape, index_map, pipeline_mode=pl.Buffered(1))


def _rms(x, g):
    return x * lax.rsqrt(jnp.mean(x * x, axis=-1, keepdims=True) + EPS) * g


def _sigmoid(x):
    return 1.0 / (1.0 + jnp.exp(-x))


def _dot(a, b):
    return jnp.dot(a, b, preferred_element_type=F32)


def _dot_nt(a, b):
    return lax.dot_general(a, b, (((1,), (1,)), ((), ())), preferred_element_type=F32)


def _ffn_kernel(x_ref, g_ref, wgu_ref, wd_ref, gf_ref, o_ref, acc_ref, *, d_ff, final_norm):
    x = x_ref[...]
    xb = _rms(x, g_ref[...]).astype(BF16)
    for j in range(d_ff // FFN_CHUNK):
        lo = j * FFN_CHUNK
        a = _dot(xb, wgu_ref[:, lo:lo + FFN_CHUNK])
        b = _dot(xb, wgu_ref[:, d_ff + lo:d_ff + lo + FFN_CHUNK])
        hid = (a * _sigmoid(a) * b).astype(BF16)
        part = _dot(hid, wd_ref[lo:lo + FFN_CHUNK, :])
        if j == 0:
            acc_ref[...] = part
        else:
            acc_ref[...] += part
    y = x + 0.5 * acc_ref[...]
    if final_norm:
        y = _rms(y, gf_ref[...])
    o_ref[...] = y


def _ffn(x, g, wgu, wd, g_final, final_norm):
    m, d = x.shape
    d_ff = wd.shape[0]
    tm = min(ROW_TILE, m)
    return pl.pallas_call(
        functools.partial(_ffn_kernel, d_ff=d_ff, final_norm=final_norm),
        grid=(m // tm,),
        in_specs=[pl.BlockSpec((tm, d), lambda i: (i, 0)),
                  _resident((1, d), lambda i: (0, 0)),
                  _resident((d, 2 * d_ff), lambda i: (0, 0)),
                  _resident((d_ff, d), lambda i: (0, 0)),
                  _resident((1, d), lambda i: (0, 0))],
        out_specs=pl.BlockSpec((tm, d), lambda i: (i, 0)),
        out_shape=jax.ShapeDtypeStruct((m, d), F32),
        scratch_shapes=[pltpu.VMEM((tm, d), F32)],
        compiler_params=_cparams(1),
        name="ffn",
    )(x, g, wgu, wd, g_final)


def _proj_kernel(h_ref, g_ref, w_ref, z_ref, bc_ref, sgc_ref, sga_ref, kt_ref, vt_ref, kit_ref, *refs,
                 c, a, d, prompt):
    u = _rms(h_ref[...], g_ref[...]).astype(BF16)

    def mm(lo, hi):
        return _dot(u, w_ref[:, lo:hi])

    z_ref[...] = mm(2 * c, 3 * c) * mm(0, c)
    bc_ref[...] = mm(c, 2 * c)
    o = 3 * c
    q = mm(o, o + a)
    k = mm(o + a, o + 2 * a)
    v = mm(o + 2 * a, o + 3 * a)
    o += 3 * a
    qi = mm(o, o + IDX_HEADS * IDX_DIM) * IDX_DIM ** -0.5
    o += IDX_HEADS * IDX_DIM
    ki = mm(o, o + LANES)
    wi = mm(o + LANES, o + 2 * LANES) * IDX_HEADS ** -0.5
    o += 2 * LANES
    sgc_ref[...] = _sigmoid(mm(o, o + d))
    sga_ref[...] = _sigmoid(mm(o + d, o + 2 * d))
    kt = k.T
    vt = v.T
    kt_ref[0] = kt
    vt_ref[0] = vt
    kit_ref[0] = ki.T[:IDX_DIM, :]
    if prompt:
        kb_ref, ki2_ref, vtb_ref, qt_ref, qit_ref, wit_ref = refs
        kb_ref[...] = k.astype(BF16)
        ki2_ref[...] = (ki + pltpu.roll(ki, IDX_DIM, 1)).astype(BF16)
        _, n_sub, n_pairs, v_rows, t = vtb_ref.shape
        for j in range(n_sub):
            for pr in range(n_pairs):
                vtb_ref[0, j, pr, :LANES, :] = vt[pr * LANES:(pr + 1) * LANES, j * t:(j + 1) * t].astype(BF16)
                vtb_ref[0, j, pr, LANES:, :] = jnp.ones((v_rows - LANES, t), BF16)
        qt_ref[0] = (q * (HEAD_DIM ** -0.5 * LOG2E)).T.astype(BF16)
        qit_ref[0] = qi.T.astype(BF16)
        wit_ref[0] = wi.T[:IDX_HEADS, :]
    else:
        q_ref, qi_ref, wi_ref, k_ref, v_ref, ki_ref = refs
        q_ref[...] = (q * (HEAD_DIM ** -0.5 * LOG2E)).astype(BF16)
        qi_ref[...] = qi.astype(BF16)
        wi_ref[...] = wi[:, :IDX_HEADS]
        k_ref[...] = k
        v_ref[...] = v
        ki_ref[...] = ki[:, :IDX_DIM]


def _proj(h, g, w, batch, prompt, att_tile=None):
    m, d = h.shape
    n_cols = w.shape[1]
    c = a = d // 2
    seq = m // batch
    tm = min(ROW_TILE, seq)
    tiles_per_seq = seq // tm
    row = lambda i: (i, 0)
    fmaj = lambda i: (i // tiles_per_seq, 0, i % tiles_per_seq)
    chunked = lambda i: (i // tiles_per_seq, i % tiles_per_seq, 0, 0)

    def nat(cols, dtype):
        return jax.ShapeDtypeStruct((m, cols), dtype), pl.BlockSpec((tm, cols), row)

    def fm(rows, dtype):
        return jax.ShapeDtypeStruct((batch, rows, seq), dtype), pl.BlockSpec((1, rows, tm), fmaj)

    outs = [nat(c, F32),
            nat(c, F32),
            nat(d, F32),
            nat(d, F32),
            fm(a, F32),
            fm(a, F32),
            fm(IDX_DIM, F32)]
    if prompt:
        assert tm % att_tile == 0
        outs += [nat(a, BF16),
                 nat(LANES, BF16),
                 (jax.ShapeDtypeStruct((batch, seq // att_tile, a // LANES, LANES + PACKED_ROWS, att_tile), BF16),
                  pl.BlockSpec((1, tm // att_tile, a // LANES, LANES + PACKED_ROWS, att_tile),
                               lambda i: (i // tiles_per_seq, i % tiles_per_seq, 0, 0, 0))),
                 fm(a, BF16),
                 fm(a, BF16),
                 fm(IDX_HEADS, F32)]
    else:
        outs += [nat(a, BF16), nat(a, BF16), nat(IDX_HEADS, F32), nat(a, F32), nat(a, F32), nat(IDX_DIM, F32)]
    return pl.pallas_call(
        functools.partial(_proj_kernel, c=c, a=a, d=d, prompt=prompt),
        grid=(m // tm,),
        in_specs=[pl.BlockSpec((tm, d), row),
                  _resident((1, d), lambda i: (0, 0)),
                  _resident((d, n_cols), lambda i: (0, 0))],
        out_specs=[o[1] for o in outs],
        out_shape=[o[0] for o in outs],
        compiler_params=_cparams(1),
        name="proj",
    )(h, g, w)


def _sortable(x):
    x = jnp.where(x == 0.0, 0.0, x)
    b = lax.bitcast_convert_type(x, I32)
    return b ^ ((b >> 31) & I32(INT_MAX))


def _store_keys(keys_ref, hi_ref, lo_ref, ci, scores):
    k = _sortable(scores)
    keys_ref[ci] = k
    hi_ref[ci] = (k >> HALF_BITS).astype(I16)
    lo_ref[ci] = ((k & HALF_MASK) - HALF_BIAS).astype(I16)


def _count(keys_ref, n_chunks, pred):
    kc, nq = keys_ref.shape[1:]

    def body(ci, acc):
        hit = jnp.where(pred(keys_ref[ci], ci), 1.0, 0.0)
        return acc + jnp.sum(hit.reshape(kc // SUBLANES, SUBLANES, nq), axis=0)

    acc = lax.fori_loop(0, n_chunks, body, jnp.zeros((SUBLANES, nq), F32))
    return jnp.sum(acc, axis=0, keepdims=True)


def _count16(x_ref, n_chunks, bound, strict=False):
    kc, nq = x_ref.shape[1:]
    b16 = bound.astype(I16)

    def body(j, acc):
        for ci in (2 * j, 2 * j + 1):
            x = x_ref[ci]
            hit = jnp.where(x > b16 if strict else x >= b16, jnp.int16(1), jnp.int16(0))
            for r in range(kc // PACKED_ROWS):
                acc = acc + hit[r * PACKED_ROWS:(r + 1) * PACKED_ROWS, :]
        return acc

    acc = lax.fori_loop(0, (n_chunks + 1) // 2, body, jnp.zeros((PACKED_ROWS, nq), I16))
    return jnp.sum(acc.astype(I32), axis=0, keepdims=True).astype(F32)


def _park_chunk(hi_ref, lo_ref, ci):
    hi_ref[ci] = jnp.full(hi_ref.shape[1:], -HALF_BIAS, I16)
    lo_ref[ci] = jnp.full(lo_ref.shape[1:], -HALF_BIAS, I16)


def _radix16(x_ref, n_chunks, rank):
    nq = x_ref.shape[2]

    def bit_step(it, prefix):
        cand = prefix | lax.shift_left(I32(1), HALF_BITS - 1 - it)
        cnt = _count16(x_ref, n_chunks, cand - HALF_BIAS)
        return jnp.where(cnt >= rank, cand, prefix)

    return lax.fori_loop(0, HALF_BITS, bit_step, jnp.zeros((1, nq), I32))


def _select_threshold(keys_ref, hi_ref, lo_ref, n_chunks, top_k, tie_ref):
    _, kc, nq = keys_ref.shape
    kf = float(top_k)

    hi = _radix16(hi_ref, n_chunks, kf) - HALF_BIAS
    above = _count16(hi_ref, n_chunks, hi, strict=True)
    hi16 = hi.astype(I16)

    def keep_equal_high(ci, carry):
        lo_ref[ci] = jnp.where(hi_ref[ci] == hi16, lo_ref[ci], jnp.int16(-HALF_BIAS))
        return carry

    lax.fori_loop(0, n_chunks, keep_equal_high, 0)
    lo_bits = _radix16(lo_ref, n_chunks, kf - above)
    thr = jnp.left_shift(hi, HALF_BITS) | lo_bits
    cnt_gt = above + _count16(lo_ref, n_chunks, lo_bits - HALF_BIAS, strict=True)
    cnt_ge = above + jnp.where(lo_bits == 0, _count16(hi_ref, n_chunks, hi) - above,
                               _count16(lo_ref, n_chunks, lo_bits - HALF_BIAS))
    tie_ref[...] = jnp.full((1, nq), INT_MAX, I32)

    excess_ties = jnp.max(cnt_ge) > kf

    @pl.when(excess_ties)
    def _():
        need = kf - cnt_gt
        row = lax.broadcasted_iota(I32, (kc, nq), 0)
        n_bits = max(1, (keys_ref.shape[0] * kc - 1).bit_length())

        def idx_step(it, j):
            cand = j | lax.shift_left(I32(1), n_bits - 1 - it)
            cnt = _count(keys_ref, n_chunks,
                         lambda kk, ci: jnp.where(kk == thr, ci * kc + row, INT_MAX) < cand)
            return jnp.where(cnt < need, cand, j)

        tie_ref[...] = lax.fori_loop(0, n_bits, idx_step, jnp.zeros((1, nq), I32))

    return thr, excess_ties


def _selected(kk, key_index, thr, tie):
    return jnp.logical_or(kk > thr, jnp.logical_and(kk == thr, key_index <= tie))


def _head_rows(x_ref, h):
    rows = x_ref[0, h * HEAD_DIM:(h + 1) * HEAD_DIM, :]
    parts = [jnp.zeros_like(rows)] * HEADS_PER_VREG
    parts[h % HEADS_PER_VREG] = rows
    return jnp.concatenate(parts, axis=0)


def _attn_kernel(qt_ref, qit_ref, wit_ref, k_ref, vt_ref, ki2_ref, bias0_ref, bias1_ref, o_ref,
                 keys_ref, hi_ref, lo_ref, qm_ref, qim_ref, m_ref, alpha_ref, acc_ref, s_ref, p_ref,
                 tie_ref, *, top_k):
    t = qt_ref.shape[2]
    i = pl.program_id(1)
    key_row = lax.broadcasted_iota(I32, (t, t), 0)
    query_col = lax.broadcasted_iota(I32, (t, t), 1)
    causal = key_row <= query_col

    for h in range(N_HEADS):
        qm_ref[h] = _head_rows(qt_ref, h)
        qim_ref[h] = _head_rows(qit_ref, h)
    w = wit_ref[0]

    def score_chunk(ci, diagonal):
        kic = ki2_ref[0, ci]
        sc = jnp.zeros((t, t), F32)
        for h in range(IDX_HEADS):
            sc = sc + w[h:h + 1, :] * jnp.maximum(_dot(kic, qim_ref[h]), 0.0)
        if diagonal:
            sc = jnp.where(causal, sc, NEG_INF)
        _store_keys(keys_ref, hi_ref, lo_ref, ci, sc)

    def score_body(j, carry):
        score_chunk(2 * j, False)
        score_chunk(2 * j + 1, False)
        return carry

    lax.fori_loop(0, i // 2, score_body, 0)

    @pl.when(i % 2 == 1)
    def _():
        score_chunk(i - 1, False)

    score_chunk(i, True)
    _park_chunk(hi_ref, lo_ref, i + 1)

    thr, excess_ties = _select_threshold(keys_ref, hi_ref, lo_ref, i + 1, top_k, tie_ref)

    def store_masks(selected):
        def mask_chunk(ci, diagonal):
            sel = selected(keys_ref[ci], ci)
            if diagonal:
                sel = jnp.logical_and(sel, causal)
            keys_ref[ci] = lax.bitcast_convert_type(jnp.where(sel, 0.0, NEG_INF), I32)

        def mask_body(ci, carry):
            mask_chunk(ci, False)
            return carry

        lax.fori_loop(0, i, mask_body, 0)
        mask_chunk(i, True)

    @pl.when(excess_ties)
    def _():
        tie = tie_ref[...]
        store_masks(lambda kk, ci: _selected(kk, ci * t + key_row, thr, tie))

    @pl.when(jnp.logical_not(excess_ties))
    def _():
        store_masks(lambda kk, ci: kk >= thr)

    m_ref[...] = jnp.full(m_ref.shape, NEG_INF, F32)
    acc_ref[...] = jnp.zeros(acc_ref.shape, F32)

    def attend(units):
        def logits(h):
            pair = slice((h // HEADS_PER_VREG) * LANES, (h // HEADS_PER_VREG + 1) * LANES)
            top = None
            for u, (ci, mode) in enumerate(units):
                add = lax.bitcast_convert_type(keys_ref[ci], F32)
                if mode == "prev":
                    add = add + bias1_ref[h]
                elif mode == "diag":
                    add = add + bias0_ref[h]
                s = _dot(k_ref[0, ci, :, pair], qm_ref[h]) + add
                s_ref[u, h] = s
                s_max = jnp.max(s, axis=0, keepdims=True)
                top = s_max if top is None else jnp.maximum(top, s_max)
            m_old = m_ref[h]
            m_new = jnp.maximum(m_old, top)
            alpha_ref[h] = jnp.exp2(m_old - m_new)
            m_ref[h] = m_new

        def weights(h):
            for u in range(len(units)):
                p_ref[u, h] = jnp.exp2((s_ref[u, h] - m_ref[h]).astype(BF16))

        def accumulate(h):
            acc = alpha_ref[h] * acc_ref[h]
            for u, (ci, _) in enumerate(units):
                acc = acc + _dot(vt_ref[0, ci, h // HEADS_PER_VREG], p_ref[u, h])
            acc_ref[h] = acc

        for stage in (logits, weights, accumulate):
            for h in range(N_HEADS):
                stage(h)

    n_far = jnp.maximum(i - 1, 0)

    def far_body(j, carry):
        attend(((2 * j, "far"), (2 * j + 1, "far")))
        return carry

    lax.fori_loop(0, n_far // 2, far_body, 0)

    @pl.when(n_far % 2 == 1)
    def _():
        attend(((n_far - 1, "far"),))

    @pl.when(i >= 1)
    def _():
        attend(((i - 1, "prev"), (i, "diag")))

    @pl.when(i == 0)
    def _():
        attend(((i, "diag"),))

    for pr in range(N_HEADS // HEADS_PER_VREG):
        parts = []
        for hh in range(HEADS_PER_VREG):
            h = pr * HEADS_PER_VREG + hh
            parts.append(acc_ref[h, hh * HEAD_DIM:(hh + 1) * HEAD_DIM, :] / acc_ref[h, LANES:LANES + 1, :])
        o_ref[:, pr * LANES:(pr + 1) * LANES] = jnp.concatenate(parts, axis=0).T.astype(BF16)


def _prompt_attention(qt, qit, wit, kb, vtb5, ki2, bias0, bias1, top_k):
    batch, n_chunks, n_pairs, v_rows, t = vtb5.shape
    a = n_pairs * LANES
    seq = n_chunks * t
    qmap = lambda b, i: (b, 0, i)
    bmap = lambda b, i: (b, 0, 0, 0)
    cmap = lambda b, i: (0, 0, 0)
    kb4 = kb.reshape(batch, n_chunks, t, a)
    ki24 = ki2.reshape(batch, n_chunks, t, LANES)
    return pl.pallas_call(
        functools.partial(_attn_kernel, top_k=top_k),
        grid=(batch, n_chunks),
        in_specs=[pl.BlockSpec((1, a, t), qmap),
                  pl.BlockSpec((1, a, t), qmap),
                  pl.BlockSpec((1, IDX_HEADS, t), qmap),
                  _resident((1, n_chunks, t, a), bmap),
                  _resident((1, n_chunks, n_pairs, v_rows, t), lambda b, i: (b, 0, 0, 0, 0)),
                  _resident((1, n_chunks, t, LANES), bmap),
                  _resident((N_HEADS, t, t), cmap),
                  _resident((N_HEADS, t, t), cmap)],
        out_specs=pl.BlockSpec((t, a), lambda b, i: (b * n_chunks + i, 0)),
        out_shape=jax.ShapeDtypeStruct((batch * seq, a), BF16),
        scratch_shapes=[pltpu.VMEM((n_chunks, t, t), I32),
                        pltpu.VMEM((n_chunks + 1, t, t), I16),
                        pltpu.VMEM((n_chunks + 1, t, t), I16),
                        pltpu.VMEM((N_HEADS, LANES, t), BF16),
                        pltpu.VMEM((IDX_HEADS, LANES, t), BF16),
                        pltpu.VMEM((N_HEADS, 1, t), F32),
                        pltpu.VMEM((N_HEADS, 1, t), F32),
                        pltpu.VMEM((N_HEADS, v_rows, t), F32),
                        pltpu.VMEM((2, N_HEADS, t, t), F32),
                        pltpu.VMEM((2, N_HEADS, t, t), BF16),
                        pltpu.VMEM((1, t), I32)],
        compiler_params=_cparams(2),
        name="prompt_attn",
    )(qt, qit, wit, kb4, vtb5, ki24, bias0, bias1)


def _bias_of_dist(dist, rb_ref, h):
    n = jnp.maximum(dist, 0)
    max_exact = N_BUCKETS // 2
    nf = jnp.maximum(n, 1).astype(F32)
    large = max_exact + (jnp.log(nf / max_exact) / math.log(REL_MAX_DIST / max_exact)
                         * (N_BUCKETS - max_exact)).astype(I32)
    large = jnp.minimum(large, N_BUCKETS - 1)
    bucket = jnp.where(n < max_exact, n, large)
    out = jnp.zeros(dist.shape, F32)
    for nb in range(N_BUCKETS):
        out = jnp.where(bucket == nb, rb_ref[nb, h], out)
    return (out - rb_ref[N_BUCKETS - 1, h]) * LOG2E


def _bias_kernel(rb_ref, b0_ref, b1_ref, bs_ref, *, t, past):
    key_row = lax.broadcasted_iota(I32, (t, t), 0)
    query_col = lax.broadcasted_iota(I32, (t, t), 1)
    key_lane = lax.broadcasted_iota(I32, (1, bs_ref.shape[1]), 1)
    for h in range(N_HEADS):
        b0_ref[h] = _bias_of_dist(query_col - key_row, rb_ref, h)
        b1_ref[h] = _bias_of_dist(t + query_col - key_row, rb_ref, h)
        bs_ref[h:h + 1, :] = _bias_of_dist(past - key_lane, rb_ref, h)


def _bias_tables(rel_bias, t, past, n_keys_s):
    return pl.pallas_call(
        functools.partial(_bias_kernel, t=t, past=past),
        in_specs=[pl.BlockSpec(memory_space=pltpu.SMEM)],
        out_shape=[jax.ShapeDtypeStruct((N_HEADS, t, t), F32),
                   jax.ShapeDtypeStruct((N_HEADS, t, t), F32),
                   jax.ShapeDtypeStruct((N_HEADS, n_keys_s), F32)],
        compiler_params=pltpu.CompilerParams(vmem_limit_bytes=VMEM_LIMIT_BYTES),
        name="bias_tables",
    )(rel_bias)


def _page_specs(layer, n_pages, rows, page, samples_per_step=1):
    def spec(g, p):
        return pl.BlockSpec((1, 1, rows, page),
                            lambda b, pt: (layer, pt[(b * samples_per_step + g) * n_pages + p], 0, 0))
    return [spec(g, p) for g in range(samples_per_step) for p in range(n_pages)]


def _sscore_kernel(pt_ref, qi_ref, wi_ref, *refs, n_pages):
    ik_refs, o_ref = refs[:-1], refs[-1]
    page = ik_refs[0].shape[3]
    for j, ik_ref in enumerate(ik_refs):
        g, p = divmod(j, n_pages)
        dots = _dot(qi_ref[g], ik_ref[0, 0].astype(BF16))
        o_ref[g, :, p * page:(p + 1) * page] = jnp.sum(wi_ref[g] * jnp.maximum(dots, 0.0),
                                                       axis=0, keepdims=True)


def _sample_scores(layer, page_table_flat, qi8, wi8, idx_t, n_pages):
    db = qi8.shape[0]
    page = idx_t.shape[3]
    g = math.gcd(db, SCORE_SAMPLES_PER_STEP)
    per_step = lambda b, pt: (b, 0, 0)
    return pl.pallas_call(
        functools.partial(_sscore_kernel, n_pages=n_pages),
        grid_spec=pltpu.PrefetchScalarGridSpec(
            num_scalar_prefetch=1,
            grid=(db // g,),
            in_specs=[pl.BlockSpec((g, IDX_HEADS, IDX_DIM), per_step),
                      pl.BlockSpec((g, IDX_HEADS, 1), per_step)]
                     + _page_specs(layer, n_pages, IDX_DIM, page, g),
            out_specs=pl.BlockSpec((g, 1, n_pages * page), per_step)),
        out_shape=jax.ShapeDtypeStruct((db, 1, n_pages * page), F32),
        compiler_params=_cparams(1),
        name="sample_scores",
    )(page_table_flat, qi8, wi8, *([idx_t] * (g * n_pages)))


def _sselect_kernel(sc_ref, qi_ref, kin_ref, wi_ref, o_ref, keys_ref, hi_ref, lo_ref, tie_ref, *, top_k, past):
    n_chunks, page, db = keys_ref.shape
    n_pages = n_chunks - 1
    for p in range(n_pages):
        _store_keys(keys_ref, hi_ref, lo_ref, p, sc_ref[:, p * page:(p + 1) * page].T)
    kin = kin_ref[...].astype(BF16).astype(F32)
    qif = qi_ref[...].astype(F32)
    w = wi_ref[...]
    s_new = jnp.zeros((db, 1), F32)
    for h in range(IDX_HEADS):
        qh = qif[:, h * IDX_DIM:(h + 1) * IDX_DIM]
        s_new = s_new + w[:, h:h + 1] * jnp.maximum(jnp.sum(qh * kin, axis=-1, keepdims=True), 0.0)
    lane = lax.broadcasted_iota(I32, (db, page), 1)
    _store_keys(keys_ref, hi_ref, lo_ref, n_pages, jnp.where(lane == 0, s_new, NEG_INF).T)
    _park_chunk(hi_ref, lo_ref, n_chunks)

    thr, _ = _select_threshold(keys_ref, hi_ref, lo_ref, n_chunks, top_k, tie_ref)
    tie = tie_ref[...]
    row = lax.broadcasted_iota(I32, (page, db), 0)
    for p in range(n_chunks):
        key_index = p * page + row
        sel = jnp.logical_and(key_index <= past, _selected(keys_ref[p], key_index, thr, tie))
        o_ref[:, p * page:(p + 1) * page] = jnp.where(sel, 0.0, NEG_INF).T


def _sample_select(scores, qi, ki_new, wi, top_k, past, page):
    db = scores.shape[0]
    n_chunks = past // page + 1
    return pl.pallas_call(
        functools.partial(_sselect_kernel, top_k=top_k, past=past),
        out_shape=jax.ShapeDtypeStruct((db, n_chunks * page), F32),
        scratch_shapes=[pltpu.VMEM((n_chunks, page, db), I32),
                        pltpu.VMEM((n_chunks + 1, page, db), I16),
                        pltpu.VMEM((n_chunks + 1, page, db), I16),
                        pltpu.VMEM((1, db), I32)],
        compiler_params=pltpu.CompilerParams(vmem_limit_bytes=VMEM_LIMIT_BYTES),
        name="sample_select",
    )(scores, qi, ki_new, wi)


def _sattn_kernel(pt_ref, q_ref, mask_ref, bias_ref, kn_ref, vn_ref, *refs):
    n_pages = (len(refs) - 1) // 2
    k_refs, v_refs, o_ref = refs[:n_pages], refs[n_pages:2 * n_pages], refs[-1]
    a = q_ref.shape[2]
    page = k_refs[0].shape[3]
    q = q_ref[0]
    lane_head = lax.shift_right_logical(lax.broadcasted_iota(I32, (N_HEADS, a), 1),
                                        HEAD_DIM.bit_length() - 1)
    own = lane_head == lax.broadcasted_iota(I32, (N_HEADS, a), 0)
    qf = q.astype(F32)
    q_heads = jnp.where(own, qf, 0.0).astype(BF16)
    parts = [_dot(q_heads, k_ref[0, 0].astype(BF16)) for k_ref in k_refs]
    kn = kn_ref[0].astype(BF16).astype(F32)
    g_new = jnp.sum(jnp.where(own, qf * kn, 0.0), axis=-1, keepdims=True)
    parts.append(jnp.where(lax.broadcasted_iota(I32, (N_HEADS, page), 1) == 0, g_new, 0.0))
    s = jnp.concatenate(parts, axis=1) + bias_ref[...] + mask_ref[0]
    p = jnp.exp2(s - jnp.max(s, axis=-1, keepdims=True))
    l = jnp.sum(p, axis=-1, keepdims=True)
    new_col = n_pages * page
    acc = p[:, new_col:new_col + 1] * vn_ref[0].astype(BF16).astype(F32)
    for pg, v_ref in enumerate(v_refs):
        acc = acc + _dot_nt(p[:, pg * page:(pg + 1) * page].astype(BF16), v_ref[0, 0].astype(BF16))
    o_ref[0] = jnp.sum(jnp.where(own, acc / l, 0.0), axis=0, keepdims=True).astype(BF16)


def _sample_attention(layer, page_table_flat, q3, mask3, bias_s, k_new3, v_new3, kt_cache, vt_cache, n_pages):
    db, _, a = q3.shape
    page = kt_cache.shape[3]
    n_keys = mask3.shape[2]
    per_sample = lambda b, pt: (b, 0, 0)
    return pl.pallas_call(
        _sattn_kernel,
        grid_spec=pltpu.PrefetchScalarGridSpec(
            num_scalar_prefetch=1,
            grid=(db,),
            in_specs=[pl.BlockSpec((1, 1, a), per_sample),
                      pl.BlockSpec((1, 1, n_keys), per_sample),
                      _resident((N_HEADS, n_keys), lambda b, pt: (0, 0)),
                      pl.BlockSpec((1, 1, a), per_sample),
                      pl.BlockSpec((1, 1, a), per_sample)]
                     + _page_specs(layer, n_pages, a, page) + _page_specs(layer, n_pages, a, page),
            out_specs=pl.BlockSpec((1, 1, a), per_sample)),
        out_shape=jax.ShapeDtypeStruct((db, 1, a), BF16),
        compiler_params=_cparams(1),
        name="sample_attn",
    )(page_table_flat, q3, mask3, bias_s, k_new3, v_new3, *([kt_cache] * n_pages), *([vt_cache] * n_pages))


def _merge_kernel(h_ref, z_ref, zp1_ref, zp2_ref, prev_ref, bc_ref, sgc_ref, sga_ref, o_ref, cw_ref,
                  woc_ref, woa_ref, wout_ref, x_ref, *, tiles_per_seq, sequence):
    z = z_ref[...]
    if sequence:
        first = (pl.program_id(0) % tiles_per_seq) == 0
        before1 = jnp.where(first, prev_ref[0, 1:2, :], zp1_ref[7:8, :])
        before2 = jnp.where(first, prev_ref[0, 0:1, :], zp1_ref[6:7, :])
        rowi = lax.broadcasted_iota(I32, z.shape, 0)
        z1 = jnp.where(rowi == 0, before1, pltpu.roll(z, 1, 0))
        z2 = jnp.where(rowi == 0, before2, jnp.where(rowi == 1, before1, pltpu.roll(z, 2, 0)))
    else:
        z1 = zp1_ref[...]
        z2 = zp2_ref[...]
    conv = cw_ref[0:1, :] * z2 + cw_ref[1:2, :] * z1 + cw_ref[2:3, :] * z
    y_conv = _dot((bc_ref[...] * conv).astype(BF16), woc_ref[...])
    y_attn = _dot(o_ref[...], woa_ref[...])
    merged = sgc_ref[...] * y_conv + sga_ref[...] * y_attn
    x_ref[...] = h_ref[...] + _dot(merged.astype(BF16), wout_ref[...])


def _merge(h, z, zp1, zp2, prev, bc, sgc, sga, o, cw, woc, woa, wout, batch, sequence):
    m, d = h.shape
    c = z.shape[1]
    seq = m // batch
    tm = min(ROW_TILE, seq) if sequence else min(ROW_TILE, m)
    tiles_per_seq = seq // tm if sequence else 1
    row = lambda i: (i, 0)
    const = lambda i: (0, 0)
    if sequence:
        halo = tm // SUBLANES
        zp1_spec = pl.BlockSpec((SUBLANES, c), lambda i: (jnp.maximum(i * halo - 1, 0), 0))
        zp2_spec = pl.BlockSpec((SUBLANES, c), lambda i: (0, 0))
        prev_spec = pl.BlockSpec((1, CONV_W - 1, c), lambda i: (i // tiles_per_seq, 0, 0))
    else:
        zp1_spec = pl.BlockSpec((tm, c), row)
        zp2_spec = pl.BlockSpec((tm, c), row)
        prev_spec = pl.BlockSpec((1, CONV_W - 1, c), lambda i: (0, 0, 0))
    return pl.pallas_call(
        functools.partial(_merge_kernel, tiles_per_seq=tiles_per_seq, sequence=sequence),
        grid=(m // tm,),
        in_specs=[pl.BlockSpec((tm, d), row),
                  pl.BlockSpec((tm, c), row),
                  zp1_spec, zp2_spec, prev_spec,
                  pl.BlockSpec((tm, c), row),
                  pl.BlockSpec((tm, d), row),
                  pl.BlockSpec((tm, d), row),
                  pl.BlockSpec((tm, o.shape[1]), row),
                  _resident((CONV_W, c), const),
                  _resident(woc.shape, const),
                  _resident(woa.shape, const),
                  _resident(wout.shape, const)],
        out_specs=pl.BlockSpec((tm, d), row),
        out_shape=jax.ShapeDtypeStruct((m, d), F32),
        compiler_params=_cparams(1),
        name="merge",
    )(h, z, zp1, zp2, prev, bc, sgc, sga, o, cw, woc, woa, wout)


def _arrange_w_in(w, d):
    c = a = d // 2
    o = 3 * c + 3 * a + IDX_HEADS * IDX_DIM
    head = w[:, :o]
    ki = jnp.pad(w[:, o:o + IDX_DIM], ((0, 0), (0, LANES - IDX_DIM)))
    wi = jnp.pad(w[:, o + IDX_DIM:o + IDX_DIM + IDX_HEADS], ((0, 0), (0, LANES - IDX_HEADS)))
    gates = w[:, o + IDX_DIM + IDX_HEADS:]
    return jnp.concatenate([head, ki, wi, gates], axis=1).astype(BF16)


def kernel(x_prompt, x_sample, cache_k, cache_v, cache_idx_k, state_conv, page_table, rel_bias,
           norm_ffn1, w_ffn1_in, w_ffn1_out, norm_mix, w_in, conv_w, w_o_conv, w_o_attn, w_out,
           norm_ffn2, w_ffn2_in, w_ffn2_out, norm_final):
    batch, seq, d = x_prompt.shape
    db, dec_seq, _ = x_sample.shape
    assert dec_seq == 1
    depth = w_in.shape[0]
    c = a = d // 2
    n_pool, page = cache_k.shape[1:3]
    n_pages = page_table.shape[1]
    past = n_pages * page
    n_keys_s = past + page
    t = min(ATT_TILE, seq)
    assert seq % t == 0 and seq % min(ROW_TILE, seq) == 0 and t >= REL_MAX_DIST
    top_k_p = min(TOPK_MAX, seq // 4)
    top_k_s = min(TOPK_MAX, (past + dec_seq) // 4)

    bias0, bias1, bias_s = _bias_tables(rel_bias, t, past, n_keys_s)
    pt_flat = page_table.reshape(-1)
    g_final = norm_final.reshape(1, d)
    conv_zero = jnp.zeros((batch, CONV_W - 1, c), F32)
    kt_cache = jnp.transpose(cache_k, (0, 1, 3, 4, 2)).reshape(depth, n_pool, a, page)
    vt_cache = jnp.transpose(cache_v, (0, 1, 3, 4, 2)).reshape(depth, n_pool, a, page)
    idx_t = jnp.transpose(cache_idx_k, (0, 1, 3, 2))

    xp = x_prompt.reshape(batch * seq, d)
    xs = x_sample.reshape(db, d)
    outs_p = {"k": [], "v": [], "ik": [], "conv": []}
    outs_s = {"k": [], "v": [], "ik": [], "conv": []}

    for l in range(depth):
        last = l == depth - 1
        g1 = norm_ffn1[l].reshape(1, d)
        g2 = norm_ffn2[l].reshape(1, d)
        gm = norm_mix[l].reshape(1, d)
        w1_in, w1_out = w_ffn1_in[l].astype(BF16), w_ffn1_out[l].astype(BF16)
        w2_in, w2_out = w_ffn2_in[l].astype(BF16), w_ffn2_out[l].astype(BF16)
        wp = _arrange_w_in(w_in[l], d)
        woc, woa, wout = w_o_conv[l].astype(BF16), w_o_attn[l].astype(BF16), w_out[l].astype(BF16)
        cw = conv_w[l]

        hp = _ffn(xp, g1, w1_in, w1_out, g_final, False)
        (z, bc, sgc, sga, kt, vt, kit, kb, ki2, vtb, qt, qit, wit) = _proj(hp, gm, wp, batch, True, t)
        o = _prompt_attention(qt, qit, wit, kb, vtb, ki2, bias0, bias1, top_k_p)
        x2 = _merge(hp, z, z, z, conv_zero, bc, sgc, sga, o, cw, woc, woa, wout, batch, True)
        xp = _ffn(x2, g2, w2_in, w2_out, g_final, last)
        outs_p["k"].append(kt)
        outs_p["v"].append(vt)
        outs_p["ik"].append(kit)
        outs_p["conv"].append(z.reshape(batch, seq, c)[:, seq - (CONV_W - 1):])

        hs = _ffn(xs, g1, w1_in, w1_out, g_final, False)
        (z, bc, sgc, sga, kt, vt, kit, q, qi, wi, k, v, ki) = _proj(hs, gm, wp, 1, False)
        scores = _sample_scores(l, pt_flat, qi.reshape(db, IDX_HEADS, IDX_DIM),
                                wi.reshape(db, IDX_HEADS, 1), idx_t, n_pages)
        mask = _sample_select(scores.reshape(db, past), qi, ki, wi, top_k_s, past, page)
        o = _sample_attention(l, pt_flat, q.reshape(db, 1, a), mask.reshape(db, 1, n_keys_s), bias_s,
                              k.reshape(db, 1, a), v.reshape(db, 1, a), kt_cache, vt_cache, n_pages)
        prev = state_conv[l]
        x2 = _merge(hs, z, prev[:, 1], prev[:, 0], conv_zero, bc, sgc, sga, o.reshape(db, a), cw,
                    woc, woa, wout, 1, False)
        xs = _ffn(x2, g2, w2_in, w2_out, g_final, last)
        outs_s["k"].append(kt)
        outs_s["v"].append(vt)
        outs_s["ik"].append(kit)
        outs_s["conv"].append(jnp.stack([prev[:, 1], z], axis=1))

    def heads_last(parts, lead, n_pos):
        x = jnp.stack(parts).reshape(depth, -1, N_HEADS, HEAD_DIM, n_pos)
        return jnp.transpose(x, (0, 1, 4, 2, 3)).reshape((depth,) + lead + (N_HEADS, HEAD_DIM))

    def dim_last(parts, lead, n_pos):
        x = jnp.transpose(jnp.stack(parts), (0, 1, 3, 2))
        return x.reshape((depth,) + lead + (IDX_DIM,))

    return (xp.reshape(batch, seq, d),
            xs.reshape(db, dec_seq, d),
            heads_last(outs_p["k"], (batch, seq), seq),
            heads_last(outs_p["v"], (batch, seq), seq),
            dim_last(outs_p["ik"], (batch, seq), seq),
            jnp.stack(outs_p["conv"]),
            heads_last(outs_s["k"], (db, dec_seq), db),
            heads_last(outs_s["v"], (db, dec_seq), db),
            dim_last(outs_s["ik"], (db, dec_seq), db),
            jnp.stack(outs_s["conv"]))
```

```python
import functools
import math

import jax
import jax.numpy as jnp
from jax import lax
from jax.experimental import pallas as pl
from jax.experimental.pallas import tpu as pltpu

F32 = jnp.float32
BF16 = jnp.bfloat16
I32 = jnp.int32

EPS = 1e-6
NEG_INF = -1e30
HEAD_DIM = 64
N_HEADS = 8
IDX_HEADS = 8
IDX_DIM = 64
TOPK_MAX = 256
N_BUCKETS = 32
REL_MAX_DIST = 128
CONV_W = 3
LANES = 128
SUBLANES = 8
HEADS_PER_VREG = LANES // HEAD_DIM
VMEM_LIMIT_BYTES = 56 * 1024 * 1024
ROW_TILE = 512
ATT_TILE = 256
FFN_CHUNK = 256
SCORE_SAMPLES_PER_STEP = 4
INT_MAX = 2**31 - 1
LOG2E = math.log2(math.e)
I16 = jnp.int16
PACKED_ROWS = 2 * SUBLANES
HALF_BITS = 16
HALF_MASK = 2**HALF_BITS - 1
HALF_BIAS = 2**(HALF_BITS - 1)


def _cparams(n_axes):
    return pltpu.CompilerParams(dimension_semantics=("arbitrary",) * n_axes,
                                vmem_limit_bytes=VMEM_LIMIT_BYTES)


def _resident(shape, index_map):
    return pl.BlockSpec(shape, index_map, pipeline_mode=pl.Buffered(1))


def _rms(x, g):
    return x * lax.rsqrt(jnp.mean(x * x, axis=-1, keepdims=True) + EPS) * g


def _sigmoid(x):
    return 1.0 / (1.0 + jnp.exp(-x))


def _dot(a, b):
    return jnp.dot(a, b, preferred_element_type=F32)


def _dot_nt(a, b):
    return lax.dot_general(a, b, (((1,), (1,)), ((), ())), preferred_element_type=F32)


def _ffn_kernel(x_ref, g_ref, wgu_ref, wd_ref, gf_ref, o_ref, acc_ref, *, d_ff, final_norm):
    x = x_ref[...]
    xb = _rms(x, g_ref[...]).astype(BF16)
    for j in range(d_ff // FFN_CHUNK):
        lo = j * FFN_CHUNK
        a = _dot(xb, wgu_ref[:, lo:lo + FFN_CHUNK])
        b = _dot(xb, wgu_ref[:, d_ff + lo:d_ff + lo + FFN_CHUNK])
        hid = (a * _sigmoid(a) * b).astype(BF16)
        part = _dot(hid, wd_ref[lo:lo + FFN_CHUNK, :])
        if j == 0:
            acc_ref[...] = part
        else:
            acc_ref[...] += part
    y = x + 0.5 * acc_ref[...]
    if final_norm:
        y = _rms(y, gf_ref[...])
    o_ref[...] = y


def _ffn(x, g, wgu, wd, g_final, final_norm):
    m, d = x.shape
    d_ff = wd.shape[0]
    tm = min(ROW_TILE, m)
    return pl.pallas_call(
        functools.partial(_ffn_kernel, d_ff=d_ff, final_norm=final_norm),
        grid=(m // tm,),
        in_specs=[pl.BlockSpec((tm, d), lambda i: (i, 0)),
                  _resident((1, d), lambda i: (0, 0)),
                  _resident((d, 2 * d_ff), lambda i: (0, 0)),
                  _resident((d_ff, d), lambda i: (0, 0)),
                  _resident((1, d), lambda i: (0, 0))],
        out_specs=pl.BlockSpec((tm, d), lambda i: (i, 0)),
        out_shape=jax.ShapeDtypeStruct((m, d), F32),
        scratch_shapes=[pltpu.VMEM((tm, d), F32)],
        compiler_params=_cparams(1),
        name="ffn",
    )(x, g, wgu, wd, g_final)


def _proj_kernel(h_ref, g_ref, w_ref, z_ref, bc_ref, sgc_ref, sga_ref, kt_ref, vt_ref, kit_ref, *refs,
                 c, a, d, prompt):
    u = _rms(h_ref[...], g_ref[...]).astype(BF16)

    def mm(lo, hi):
        return _dot(u, w_ref[:, lo:hi])

    z_ref[...] = mm(2 * c, 3 * c) * mm(0, c)
    bc_ref[...] = mm(c, 2 * c)
    o = 3 * c
    q = mm(o, o + a)
    k = mm(o + a, o + 2 * a)
    v = mm(o + 2 * a, o + 3 * a)
    o += 3 * a
    qi = mm(o, o + IDX_HEADS * IDX_DIM) * IDX_DIM ** -0.5
    o += IDX_HEADS * IDX_DIM
    ki = mm(o, o + LANES)
    wi = mm(o + LANES, o + 2 * LANES) * IDX_HEADS ** -0.5
    o += 2 * LANES
    sgc_ref[...] = _sigmoid(mm(o, o + d))
    sga_ref[...] = _sigmoid(mm(o + d, o + 2 * d))
    kt = k.T
    vt = v.T
    kt_ref[0] = kt
    vt_ref[0] = vt
    kit_ref[0] = ki.T[:IDX_DIM, :]
    if prompt:
        kb_ref, ki2_ref, vtb_ref, qt_ref, qit_ref, wit_ref = refs
        kb_ref[...] = k.astype(BF16)
        ki2_ref[...] = (ki + pltpu.roll(ki, IDX_DIM, 1)).astype(BF16)
        _, n_sub, n_pairs, v_rows, t = vtb_ref.shape
        for j in range(n_sub):
            for pr in range(n_pairs):
                vtb_ref[0, j, pr, :LANES, :] = vt[pr * LANES:(pr + 1) * LANES, j * t:(j + 1) * t].astype(BF16)
                vtb_ref[0, j, pr, LANES:, :] = jnp.ones((v_rows - LANES, t), BF16)
        qt_ref[0] = (q * (HEAD_DIM ** -0.5 * LOG2E)).T.astype(BF16)
        qit_ref[0] = qi.T.astype(BF16)
        wit_ref[0] = wi.T[:IDX_HEADS, :]
    else:
        q_ref, qi_ref, wi_ref, k_ref, v_ref, ki_ref = refs
        q_ref[...] = (q * (HEAD_DIM ** -0.5 * LOG2E)).astype(BF16)
        qi_ref[...] = qi.astype(BF16)
        wi_ref[...] = wi[:, :IDX_HEADS]
        k_ref[...] = k
        v_ref[...] = v
        ki_ref[...] = ki[:, :IDX_DIM]


def _proj(h, g, w, batch, prompt, att_tile=None):
    m, d = h.shape
    n_cols = w.shape[1]
    c = a = d // 2
    seq = m // batch
    tm = min(ROW_TILE, seq)
    tiles_per_seq = seq // tm
    row = lambda i: (i, 0)
    fmaj = lambda i: (i // tiles_per_seq, 0, i % tiles_per_seq)
    chunked = lambda i: (i // tiles_per_seq, i % tiles_per_seq, 0, 0)

    def nat(cols, dtype):
        return jax.ShapeDtypeStruct((m, cols), dtype), pl.BlockSpec((tm, cols), row)

    def fm(rows, dtype):
        return jax.ShapeDtypeStruct((batch, rows, seq), dtype), pl.BlockSpec((1, rows, tm), fmaj)

    outs = [nat(c, F32),
            nat(c, F32),
            nat(d, F32),
            nat(d, F32),
            fm(a, F32),
            fm(a, F32),
            fm(IDX_DIM, F32)]
    if prompt:
        assert tm % att_tile == 0
        outs += [nat(a, BF16),
                 nat(LANES, BF16),
                 (jax.ShapeDtypeStruct((batch, seq // att_tile, a // LANES, LANES + PACKED_ROWS, att_tile), BF16),
                  pl.BlockSpec((1, tm // att_tile, a // LANES, LANES + PACKED_ROWS, att_tile),
                               lambda i: (i // tiles_per_seq, i % tiles_per_seq, 0, 0, 0))),
                 fm(a, BF16),
                 fm(a, BF16),
                 fm(IDX_HEADS, F32)]
    else:
        outs += [nat(a, BF16), nat(a, BF16), nat(IDX_HEADS, F32), nat(a, F32), nat(a, F32), nat(IDX_DIM, F32)]
    return pl.pallas_call(
        functools.partial(_proj_kernel, c=c, a=a, d=d, prompt=prompt),
        grid=(m // tm,),
        in_specs=[pl.BlockSpec((tm, d), row),
                  _resident((1, d), lambda i: (0, 0)),
                  _resident((d, n_cols), lambda i: (0, 0))],
        out_specs=[o[1] for o in outs],
        out_shape=[o[0] for o in outs],
        compiler_params=_cparams(1),
        name="proj",
    )(h, g, w)


def _sortable(x):
    x = jnp.where(x == 0.0, 0.0, x)
    b = lax.bitcast_convert_type(x, I32)
    return b ^ ((b >> 31) & I32(INT_MAX))


def _store_keys(keys_ref, hi_ref, lo_ref, ci, scores):
    k = _sortable(scores)
    keys_ref[ci] = k
    hi_ref[ci] = (k >> HALF_BITS).astype(I16)
    lo_ref[ci] = ((k & HALF_MASK) - HALF_BIAS).astype(I16)


def _count(keys_ref, n_chunks, pred):
    kc, nq = keys_ref.shape[1:]

    def body(ci, acc):
        hit = jnp.where(pred(keys_ref[ci], ci), 1.0, 0.0)
        return acc + jnp.sum(hit.reshape(kc // SUBLANES, SUBLANES, nq), axis=0)

    acc = lax.fori_loop(0, n_chunks, body, jnp.zeros((SUBLANES, nq), F32))
    return jnp.sum(acc, axis=0, keepdims=True)


def _count16(x_ref, n_chunks, bound, strict=False):
    kc, nq = x_ref.shape[1:]
    b16 = bound.astype(I16)

    def body(j, acc):
        parts = []
        for ci in (2 * j, 2 * j + 1):
            x = x_ref[ci]
            hit = jnp.where(x > b16 if strict else x >= b16, jnp.int16(1), jnp.int16(0))
            parts += [hit[r * PACKED_ROWS:(r + 1) * PACKED_ROWS, :] for r in range(kc // PACKED_ROWS)]
        while len(parts) > 1:
            parts = [a + b for a, b in zip(parts[::2], parts[1::2])] + parts[len(parts) & ~1:]
        return acc + parts[0]

    acc = lax.fori_loop(0, (n_chunks + 1) // 2, body, jnp.zeros((PACKED_ROWS, nq), I16))
    return jnp.sum(acc.astype(I32), axis=0, keepdims=True).astype(F32)


def _park_chunk(hi_ref, lo_ref, ci):
    hi_ref[ci] = jnp.full(hi_ref.shape[1:], -HALF_BIAS, I16)
    lo_ref[ci] = jnp.full(lo_ref.shape[1:], -HALF_BIAS, I16)


def _radix16(x_ref, n_chunks, rank):
    nq = x_ref.shape[2]

    def bit_step(it, prefix):
        cand = prefix | lax.shift_left(I32(1), HALF_BITS - 1 - it)
        cnt = _count16(x_ref, n_chunks, cand - HALF_BIAS)
        return jnp.where(cnt >= rank, cand, prefix)

    return lax.fori_loop(0, HALF_BITS, bit_step, jnp.zeros((1, nq), I32))


def _select_threshold(keys_ref, hi_ref, lo_ref, n_chunks, top_k, tie_ref):
    _, kc, nq = keys_ref.shape
    kf = float(top_k)

    hi = _radix16(hi_ref, n_chunks, kf) - HALF_BIAS
    above = _count16(hi_ref, n_chunks, hi, strict=True)
    hi16 = hi.astype(I16)

    def keep_equal_high(ci, carry):
        lo_ref[ci] = jnp.where(hi_ref[ci] == hi16, lo_ref[ci], jnp.int16(-HALF_BIAS))
        return carry

    lax.fori_loop(0, n_chunks, keep_equal_high, 0)
    lo_bits = _radix16(lo_ref, n_chunks, kf - above)
    thr = jnp.left_shift(hi, HALF_BITS) | lo_bits
    cnt_gt = above + _count16(lo_ref, n_chunks, lo_bits - HALF_BIAS, strict=True)
    cnt_ge = above + jnp.where(lo_bits == 0, _count16(hi_ref, n_chunks, hi) - above,
                               _count16(lo_ref, n_chunks, lo_bits - HALF_BIAS))
    tie_ref[...] = jnp.full((1, nq), INT_MAX, I32)

    excess_ties = jnp.max(cnt_ge) > kf

    @pl.when(excess_ties)
    def _():
        need = kf - cnt_gt
        row = lax.broadcasted_iota(I32, (kc, nq), 0)
        n_bits = max(1, (keys_ref.shape[0] * kc - 1).bit_length())

        def idx_step(it, j):
            cand = j | lax.shift_left(I32(1), n_bits - 1 - it)
            cnt = _count(keys_ref, n_chunks,
                         lambda kk, ci: jnp.where(kk == thr, ci * kc + row, INT_MAX) < cand)
            return jnp.where(cnt < need, cand, j)

        tie_ref[...] = lax.fori_loop(0, n_bits, idx_step, jnp.zeros((1, nq), I32))

    return thr, excess_ties


def _selected(kk, key_index, thr, tie):
    return jnp.logical_or(kk > thr, jnp.logical_and(kk == thr, key_index <= tie))


def _head_rows(x_ref, h):
    rows = x_ref[0, h * HEAD_DIM:(h + 1) * HEAD_DIM, :]
    parts = [jnp.zeros_like(rows)] * HEADS_PER_VREG
    parts[h % HEADS_PER_VREG] = rows
    return jnp.concatenate(parts, axis=0)


def _attn_kernel(qt_ref, qit_ref, wit_ref, k_ref, vt_ref, ki2_ref, bias0_ref, bias1_ref, o_ref,
                 keys_ref, hi_ref, lo_ref, qm_ref, qim_ref, m_ref, alpha_ref, acc_ref, s_ref, p_ref,
                 tie_ref, *, top_k):
    t = qt_ref.shape[2]
    i = pl.program_id(1)
    key_row = lax.broadcasted_iota(I32, (t, t), 0)
    query_col = lax.broadcasted_iota(I32, (t, t), 1)
    causal = key_row <= query_col

    for h in range(N_HEADS):
        qm_ref[h] = _head_rows(qt_ref, h)
        qim_ref[h] = _head_rows(qit_ref, h)
    w = wit_ref[0]

    def score_chunk(ci, diagonal):
        kic = ki2_ref[0, ci]
        sc = jnp.zeros((t, t), F32)
        for h in range(IDX_HEADS):
            sc = sc + w[h:h + 1, :] * jnp.maximum(_dot(kic, qim_ref[h]), 0.0)
        if diagonal:
            sc = jnp.where(causal, sc, NEG_INF)
        _store_keys(keys_ref, hi_ref, lo_ref, ci, sc)

    def score_body(j, carry):
        score_chunk(2 * j, False)
        score_chunk(2 * j + 1, False)
        return carry

    lax.fori_loop(0, i // 2, score_body, 0)

    @pl.when(i % 2 == 1)
    def _():
        score_chunk(i - 1, False)

    score_chunk(i, True)
    _park_chunk(hi_ref, lo_ref, i + 1)

    thr, excess_ties = _select_threshold(keys_ref, hi_ref, lo_ref, i + 1, top_k, tie_ref)

    def store_masks(selected):
        def mask_chunk(ci, diagonal):
            sel = selected(keys_ref[ci], ci)
            if diagonal:
                sel = jnp.logical_and(sel, causal)
            keys_ref[ci] = lax.bitcast_convert_type(jnp.where(sel, 0.0, NEG_INF), I32)

        def mask_body(ci, carry):
            mask_chunk(ci, False)
            return carry

        lax.fori_loop(0, i, mask_body, 0)
        mask_chunk(i, True)

    @pl.when(excess_ties)
    def _():
        tie = tie_ref[...]
        store_masks(lambda kk, ci: _selected(kk, ci * t + key_row, thr, tie))

    @pl.when(jnp.logical_not(excess_ties))
    def _():
        store_masks(lambda kk, ci: kk >= thr)

    m_ref[...] = jnp.full(m_ref.shape, NEG_INF, F32)
    acc_ref[...] = jnp.zeros(acc_ref.shape, F32)

    def attend(units):
        def logits(h):
            pair = slice((h // HEADS_PER_VREG) * LANES, (h // HEADS_PER_VREG + 1) * LANES)
            top = None
            for u, (ci, mode) in enumerate(units):
                add = lax.bitcast_convert_type(keys_ref[ci], F32)
                if mode == "prev":
                    add = add + bias1_ref[h]
                elif mode == "diag":
                    add = add + bias0_ref[h]
                s = _dot(k_ref[0, ci, :, pair], qm_ref[h]) + add
                s_ref[u, h] = s
                s_max = jnp.max(s, axis=0, keepdims=True)
                top = s_max if top is None else jnp.maximum(top, s_max)
            m_old = m_ref[h]
            m_new = jnp.maximum(m_old, top)
            alpha_ref[h] = jnp.exp2(m_old - m_new)
            m_ref[h] = m_new

        def weights(h):
            for u in range(len(units)):
                p_ref[u, h] = jnp.exp2((s_ref[u, h] - m_ref[h]).astype(BF16))

        def accumulate(h):
            acc = alpha_ref[h] * acc_ref[h]
            for u, (ci, _) in enumerate(units):
                acc = acc + _dot(vt_ref[0, ci, h // HEADS_PER_VREG], p_ref[u, h])
            acc_ref[h] = acc

        for stage in (logits, weights, accumulate):
            for h in range(N_HEADS):
                stage(h)

    n_far = jnp.maximum(i - 1, 0)

    def far_body(j, carry):
        attend(((2 * j, "far"), (2 * j + 1, "far")))
        return carry

    lax.fori_loop(0, n_far // 2, far_body, 0)

    @pl.when(n_far % 2 == 1)
    def _():
        attend(((n_far - 1, "far"),))

    @pl.when(i >= 1)
    def _():
        attend(((i - 1, "prev"), (i, "diag")))

    @pl.when(i == 0)
    def _():
        attend(((i, "diag"),))

    for pr in range(N_HEADS // HEADS_PER_VREG):
        parts = []
        for hh in range(HEADS_PER_VREG):
            h = pr * HEADS_PER_VREG + hh
            parts.append(acc_ref[h, hh * HEAD_DIM:(hh + 1) * HEAD_DIM, :] / acc_ref[h, LANES:LANES + 1, :])
        o_ref[:, pr * LANES:(pr + 1) * LANES] = jnp.concatenate(parts, axis=0).T.astype(BF16)


def _prompt_attention(qt, qit, wit, kb, vtb5, ki2, bias0, bias1, top_k):
    batch, n_chunks, n_pairs, v_rows, t = vtb5.shape
    a = n_pairs * LANES
    seq = n_chunks * t
    qmap = lambda b, i: (b, 0, i)
    bmap = lambda b, i: (b, 0, 0, 0)
    cmap = lambda b, i: (0, 0, 0)
    kb4 = kb.reshape(batch, n_chunks, t, a)
    ki24 = ki2.reshape(batch, n_chunks, t, LANES)
    return pl.pallas_call(
        functools.partial(_attn_kernel, top_k=top_k),
        grid=(batch, n_chunks),
        in_specs=[pl.BlockSpec((1, a, t), qmap),
                  pl.BlockSpec((1, a, t), qmap),
                  pl.BlockSpec((1, IDX_HEADS, t), qmap),
                  _resident((1, n_chunks, t, a), bmap),
                  _resident((1, n_chunks, n_pairs, v_rows, t), lambda b, i: (b, 0, 0, 0, 0)),
                  _resident((1, n_chunks, t, LANES), bmap),
                  _resident((N_HEADS, t, t), cmap),
                  _resident((N_HEADS, t, t), cmap)],
        out_specs=pl.BlockSpec((t, a), lambda b, i: (b * n_chunks + i, 0)),
        out_shape=jax.ShapeDtypeStruct((batch * seq, a), BF16),
        scratch_shapes=[pltpu.VMEM((n_chunks, t, t), I32),
                        pltpu.VMEM((n_chunks + 1, t, t), I16),
                        pltpu.VMEM((n_chunks + 1, t, t), I16),
                        pltpu.VMEM((N_HEADS, LANES, t), BF16),
                        pltpu.VMEM((IDX_HEADS, LANES, t), BF16),
                        pltpu.VMEM((N_HEADS, 1, t), F32),
                        pltpu.VMEM((N_HEADS, 1, t), F32),
                        pltpu.VMEM((N_HEADS, v_rows, t), F32),
                        pltpu.VMEM((2, N_HEADS, t, t), F32),
                        pltpu.VMEM((2, N_HEADS, t, t), BF16),
                        pltpu.VMEM((1, t), I32)],
        compiler_params=_cparams(2),
        name="prompt_attn",
    )(qt, qit, wit, kb4, vtb5, ki24, bias0, bias1)


def _bias_of_dist(dist, rb_ref, h):
    n = jnp.maximum(dist, 0)
    max_exact = N_BUCKETS // 2
    nf = jnp.maximum(n, 1).astype(F32)
    large = max_exact + (jnp.log(nf / max_exact) / math.log(REL_MAX_DIST / max_exact)
                         * (N_BUCKETS - max_exact)).astype(I32)
    large = jnp.minimum(large, N_BUCKETS - 1)
    bucket = jnp.where(n < max_exact, n, large)
    out = jnp.zeros(dist.shape, F32)
    for nb in range(N_BUCKETS):
        out = jnp.where(bucket == nb, rb_ref[nb, h], out)
    return (out - rb_ref[N_BUCKETS - 1, h]) * LOG2E


def _bias_kernel(rb_ref, b0_ref, b1_ref, bs_ref, *, t, past):
    key_row = lax.broadcasted_iota(I32, (t, t), 0)
    query_col = lax.broadcasted_iota(I32, (t, t), 1)
    key_lane = lax.broadcasted_iota(I32, (1, bs_ref.shape[1]), 1)
    for h in range(N_HEADS):
        b0_ref[h] = _bias_of_dist(query_col - key_row, rb_ref, h)
        b1_ref[h] = _bias_of_dist(t + query_col - key_row, rb_ref, h)
        bs_ref[h:h + 1, :] = _bias_of_dist(past - key_lane, rb_ref, h)


def _bias_tables(rel_bias, t, past, n_keys_s):
    return pl.pallas_call(
        functools.partial(_bias_kernel, t=t, past=past),
        in_specs=[pl.BlockSpec(memory_space=pltpu.SMEM)],
        out_shape=[jax.ShapeDtypeStruct((N_HEADS, t, t), F32),
                   jax.ShapeDtypeStruct((N_HEADS, t, t), F32),
                   jax.ShapeDtypeStruct((N_HEADS, n_keys_s), F32)],
        compiler_params=pltpu.CompilerParams(vmem_limit_bytes=VMEM_LIMIT_BYTES),
        name="bias_tables",
    )(rel_bias)


def _page_specs(layer, n_pages, rows, page, samples_per_step=1):
    def spec(g, p):
        return pl.BlockSpec((1, 1, rows, page),
                            lambda b, pt: (layer, pt[(b * samples_per_step + g) * n_pages + p], 0, 0))
    return [spec(g, p) for g in range(samples_per_step) for p in range(n_pages)]


def _sscore_kernel(pt_ref, qi_ref, wi_ref, *refs, n_pages):
    ik_refs, o_ref = refs[:-1], refs[-1]
    page = ik_refs[0].shape[3]
    for j, ik_ref in enumerate(ik_refs):
        g, p = divmod(j, n_pages)
        dots = _dot(qi_ref[g], ik_ref[0, 0].astype(BF16))
        o_ref[g, :, p * page:(p + 1) * page] = jnp.sum(wi_ref[g] * jnp.maximum(dots, 0.0),
                                                       axis=0, keepdims=True)


def _sample_scores(layer, page_table_flat, qi8, wi8, idx_t, n_pages):
    db = qi8.shape[0]
    page = idx_t.shape[3]
    g = math.gcd(db, SCORE_SAMPLES_PER_STEP)
    per_step = lambda b, pt: (b, 0, 0)
    return pl.pallas_call(
        functools.partial(_sscore_kernel, n_pages=n_pages),
        grid_spec=pltpu.PrefetchScalarGridSpec(
            num_scalar_prefetch=1,
            grid=(db // g,),
            in_specs=[pl.BlockSpec((g, IDX_HEADS, IDX_DIM), per_step),
                      pl.BlockSpec((g, IDX_HEADS, 1), per_step)]
                     + _page_specs(layer, n_pages, IDX_DIM, page, g),
            out_specs=pl.BlockSpec((g, 1, n_pages * page), per_step)),
        out_shape=jax.ShapeDtypeStruct((db, 1, n_pages * page), F32),
        compiler_params=_cparams(1),
        name="sample_scores",
    )(page_table_flat, qi8, wi8, *([idx_t] * (g * n_pages)))


def _sselect_kernel(sc_ref, qi_ref, kin_ref, wi_ref, o_ref, keys_ref, hi_ref, lo_ref, tie_ref, *, top_k, past):
    n_chunks, page, db = keys_ref.shape
    n_pages = n_chunks - 1
    for p in range(n_pages):
        _store_keys(keys_ref, hi_ref, lo_ref, p, sc_ref[:, p * page:(p + 1) * page].T)
    kin = kin_ref[...].astype(BF16).astype(F32)
    qif = qi_ref[...].astype(F32)
    w = wi_ref[...]
    s_new = jnp.zeros((db, 1), F32)
    for h in range(IDX_HEADS):
        qh = qif[:, h * IDX_DIM:(h + 1) * IDX_DIM]
        s_new = s_new + w[:, h:h + 1] * jnp.maximum(jnp.sum(qh * kin, axis=-1, keepdims=True), 0.0)
    lane = lax.broadcasted_iota(I32, (db, page), 1)
    _store_keys(keys_ref, hi_ref, lo_ref, n_pages, jnp.where(lane == 0, s_new, NEG_INF).T)
    _park_chunk(hi_ref, lo_ref, n_chunks)

    thr, _ = _select_threshold(keys_ref, hi_ref, lo_ref, n_chunks, top_k, tie_ref)
    tie = tie_ref[...]
    row = lax.broadcasted_iota(I32, (page, db), 0)
    for p in range(n_chunks):
        key_index = p * page + row
        sel = jnp.logical_and(key_index <= past, _selected(keys_ref[p], key_index, thr, tie))
        o_ref[:, p * page:(p + 1) * page] = jnp.where(sel, 0.0, NEG_INF).T


def _sample_select(scores, qi, ki_new, wi, top_k, past, page):
    db = scores.shape[0]
    n_chunks = past // page + 1
    return pl.pallas_call(
        functools.partial(_sselect_kernel, top_k=top_k, past=past),
        out_shape=jax.ShapeDtypeStruct((db, n_chunks * page), F32),
        scratch_shapes=[pltpu.VMEM((n_chunks, page, db), I32),
                        pltpu.VMEM((n_chunks + 1, page, db), I16),
                        pltpu.VMEM((n_chunks + 1, page, db), I16),
                        pltpu.VMEM((1, db), I32)],
        compiler_params=pltpu.CompilerParams(vmem_limit_bytes=VMEM_LIMIT_BYTES),
        name="sample_select",
    )(scores, qi, ki_new, wi)


def _sattn_kernel(pt_ref, q_ref, mask_ref, bias_ref, kn_ref, vn_ref, *refs):
    n_pages = (len(refs) - 1) // 2
    k_refs, v_refs, o_ref = refs[:n_pages], refs[n_pages:2 * n_pages], refs[-1]
    a = q_ref.shape[2]
    page = k_refs[0].shape[3]
    q = q_ref[0]
    lane_head = lax.shift_right_logical(lax.broadcasted_iota(I32, (N_HEADS, a), 1),
                                        HEAD_DIM.bit_length() - 1)
    own = lane_head == lax.broadcasted_iota(I32, (N_HEADS, a), 0)
    qf = q.astype(F32)
    q_heads = jnp.where(own, qf, 0.0).astype(BF16)
    parts = [_dot(q_heads, k_ref[0, 0].astype(BF16)) for k_ref in k_refs]
    kn = kn_ref[0].astype(BF16).astype(F32)
    g_new = jnp.sum(jnp.where(own, qf * kn, 0.0), axis=-1, keepdims=True)
    parts.append(jnp.where(lax.broadcasted_iota(I32, (N_HEADS, page), 1) == 0, g_new, 0.0))
    s = jnp.concatenate(parts, axis=1) + bias_ref[...] + mask_ref[0]
    p = jnp.exp2(s - jnp.max(s, axis=-1, keepdims=True))
    l = jnp.sum(p, axis=-1, keepdims=True)
    new_col = n_pages * page
    acc = p[:, new_col:new_col + 1] * vn_ref[0].astype(BF16).astype(F32)
    for pg, v_ref in enumerate(v_refs):
        acc = acc + _dot_nt(p[:, pg * page:(pg + 1) * page].astype(BF16), v_ref[0, 0].astype(BF16))
    o_ref[0] = jnp.sum(jnp.where(own, acc / l, 0.0), axis=0, keepdims=True).astype(BF16)


def _sample_attention(layer, page_table_flat, q3, mask3, bias_s, k_new3, v_new3, kt_cache, vt_cache, n_pages):
    db, _, a = q3.shape
    page = kt_cache.shape[3]
    n_keys = mask3.shape[2]
    per_sample = lambda b, pt: (b, 0, 0)
    return pl.pallas_call(
        _sattn_kernel,
        grid_spec=pltpu.PrefetchScalarGridSpec(
            num_scalar_prefetch=1,
            grid=(db,),
            in_specs=[pl.BlockSpec((1, 1, a), per_sample),
                      pl.BlockSpec((1, 1, n_keys), per_sample),
                      _resident((N_HEADS, n_keys), lambda b, pt: (0, 0)),
                      pl.BlockSpec((1, 1, a), per_sample),
                      pl.BlockSpec((1, 1, a), per_sample)]
                     + _page_specs(layer, n_pages, a, page) + _page_specs(layer, n_pages, a, page),
            out_specs=pl.BlockSpec((1, 1, a), per_sample)),
        out_shape=jax.ShapeDtypeStruct((db, 1, a), BF16),
        compiler_params=_cparams(1),
        name="sample_attn",
    )(page_table_flat, q3, mask3, bias_s, k_new3, v_new3, *([kt_cache] * n_pages), *([vt_cache] * n_pages))


def _merge_kernel(h_ref, z_ref, zp1_ref, zp2_ref, prev_ref, bc_ref, sgc_ref, sga_ref, o_ref, cw_ref,
                  woc_ref, woa_ref, wout_ref, x_ref, *, tiles_per_seq, sequence):
    z = z_ref[...]
    if sequence:
        first = (pl.program_id(0) % tiles_per_seq) == 0
        before1 = jnp.where(first, prev_ref[0, 1:2, :], zp1_ref[7:8, :])
        before2 = jnp.where(first, prev_ref[0, 0:1, :], zp1_ref[6:7, :])
        rowi = lax.broadcasted_iota(I32, z.shape, 0)
        z1 = jnp.where(rowi == 0, before1, pltpu.roll(z, 1, 0))
        z2 = jnp.where(rowi == 0, before2, jnp.where(rowi == 1, before1, pltpu.roll(z, 2, 0)))
    else:
        z1 = zp1_ref[...]
        z2 = zp2_ref[...]
    conv = cw_ref[0:1, :] * z2 + cw_ref[1:2, :] * z1 + cw_ref[2:3, :] * z
    y_conv = _dot((bc_ref[...] * conv).astype(BF16), woc_ref[...])
    y_attn = _dot(o_ref[...], woa_ref[...])
    merged = sgc_ref[...] * y_conv + sga_ref[...] * y_attn
    x_ref[...] = h_ref[...] + _dot(merged.astype(BF16), wout_ref[...])


def _merge(h, z, zp1, zp2, prev, bc, sgc, sga, o, cw, woc, woa, wout, batch, sequence):
    m, d = h.shape
    c = z.shape[1]
    seq = m // batch
    tm = min(ROW_TILE, seq) if sequence else min(ROW_TILE, m)
    tiles_per_seq = seq // tm if sequence else 1
    row = lambda i: (i, 0)
    const = lambda i: (0, 0)
    if sequence:
        halo = tm // SUBLANES
        zp1_spec = pl.BlockSpec((SUBLANES, c), lambda i: (jnp.maximum(i * halo - 1, 0), 0))
        zp2_spec = pl.BlockSpec((SUBLANES, c), lambda i: (0, 0))
        prev_spec = pl.BlockSpec((1, CONV_W - 1, c), lambda i: (i // tiles_per_seq, 0, 0))
    else:
        zp1_spec = pl.BlockSpec((tm, c), row)
        zp2_spec = pl.BlockSpec((tm, c), row)
        prev_spec = pl.BlockSpec((1, CONV_W - 1, c), lambda i: (0, 0, 0))
    return pl.pallas_call(
        functools.partial(_merge_kernel, tiles_per_seq=tiles_per_seq, sequence=sequence),
        grid=(m // tm,),
        in_specs=[pl.BlockSpec((tm, d), row),
                  pl.BlockSpec((tm, c), row),
                  zp1_spec, zp2_spec, prev_spec,
                  pl.BlockSpec((tm, c), row),
                  pl.BlockSpec((tm, d), row),
                  pl.BlockSpec((tm, d), row),
                  pl.BlockSpec((tm, o.shape[1]), row),
                  _resident((CONV_W, c), const),
                  _resident(woc.shape, const),
                  _resident(woa.shape, const),
                  _resident(wout.shape, const)],
        out_specs=pl.BlockSpec((tm, d), row),
        out_shape=jax.ShapeDtypeStruct((m, d), F32),
        compiler_params=_cparams(1),
        name="merge",
    )(h, z, zp1, zp2, prev, bc, sgc, sga, o, cw, woc, woa, wout)


def _arrange_w_in(w, d):
    c = a = d // 2
    o = 3 * c + 3 * a + IDX_HEADS * IDX_DIM
    head = w[:, :o]
    ki = jnp.pad(w[:, o:o + IDX_DIM], ((0, 0), (0, LANES - IDX_DIM)))
    wi = jnp.pad(w[:, o + IDX_DIM:o + IDX_DIM + IDX_HEADS], ((0, 0), (0, LANES - IDX_HEADS)))
    gates = w[:, o + IDX_DIM + IDX_HEADS:]
    return jnp.concatenate([head, ki, wi, gates], axis=1).astype(BF16)


def kernel(x_prompt, x_sample, cache_k, cache_v, cache_idx_k, state_conv, page_table, rel_bias,
           norm_ffn1, w_ffn1_in, w_ffn1_out, norm_mix, w_in, conv_w, w_o_conv, w_o_attn, w_out,
           norm_ffn2, w_ffn2_in, w_ffn2_out, norm_final):
    batch, seq, d = x_prompt.shape
    db, dec_seq, _ = x_sample.shape
    assert dec_seq == 1
    depth = w_in.shape[0]
    c = a = d // 2
    n_pool, page = cache_k.shape[1:3]
    n_pages = page_table.shape[1]
    past = n_pages * page
    n_keys_s = past + page
    t = min(ATT_TILE, seq)
    assert seq % t == 0 and seq % min(ROW_TILE, seq) == 0 and t >= REL_MAX_DIST
    top_k_p = min(TOPK_MAX, seq // 4)
    top_k_s = min(TOPK_MAX, (past + dec_seq) // 4)

    bias0, bias1, bias_s = _bias_tables(rel_bias, t, past, n_keys_s)
    pt_flat = page_table.reshape(-1)
    g_final = norm_final.reshape(1, d)
    conv_zero = jnp.zeros((batch, CONV_W - 1, c), F32)
    kt_cache = jnp.transpose(cache_k, (0, 1, 3, 4, 2)).reshape(depth, n_pool, a, page)
    vt_cache = jnp.transpose(cache_v, (0, 1, 3, 4, 2)).reshape(depth, n_pool, a, page)
    idx_t = jnp.transpose(cache_idx_k, (0, 1, 3, 2))

    xp = x_prompt.reshape(batch * seq, d)
    xs = x_sample.reshape(db, d)
    outs_p = {"k": [], "v": [], "ik": [], "conv": []}
    outs_s = {"k": [], "v": [], "ik": [], "conv": []}

    for l in range(depth):
        last = l == depth - 1
        g1 = norm_ffn1[l].reshape(1, d)
        g2 = norm_ffn2[l].reshape(1, d)
        gm = norm_mix[l].reshape(1, d)
        w1_in, w1_out = w_ffn1_in[l].astype(BF16), w_ffn1_out[l].astype(BF16)
        w2_in, w2_out = w_ffn2_in[l].astype(BF16), w_ffn2_out[l].astype(BF16)
        wp = _arrange_w_in(w_in[l], d)
        woc, woa, wout = w_o_conv[l].astype(BF16), w_o_attn[l].astype(BF16), w_out[l].astype(BF16)
        cw = conv_w[l]

        hp = _ffn(xp, g1, w1_in, w1_out, g_final, False)
        (z, bc, sgc, sga, kt, vt, kit, kb, ki2, vtb, qt, qit, wit) = _proj(hp, gm, wp, batch, True, t)
        o = _prompt_attention(qt, qit, wit, kb, vtb, ki2, bias0, bias1, top_k_p)
        x2 = _merge(hp, z, z, z, conv_zero, bc, sgc, sga, o, cw, woc, woa, wout, batch, True)
        xp = _ffn(x2, g2, w2_in, w2_out, g_final, last)
        outs_p["k"].append(kt)
        outs_p["v"].append(vt)
        outs_p["ik"].append(kit)
        outs_p["conv"].append(z.reshape(batch, seq, c)[:, seq - (CONV_W - 1):])

        hs = _ffn(xs, g1, w1_in, w1_out, g_final, False)
        (z, bc, sgc, sga, kt, vt, kit, q, qi, wi, k, v, ki) = _proj(hs, gm, wp, 1, False)
        scores = _sample_scores(l, pt_flat, qi.reshape(db, IDX_HEADS, IDX_DIM),
                                wi.reshape(db, IDX_HEADS, 1), idx_t, n_pages)
        mask = _sample_select(scores.reshape(db, past), qi, ki, wi, top_k_s, past, page)
        o = _sample_attention(l, pt_flat, q.reshape(db, 1, a), mask.reshape(db, 1, n_keys_s), bias_s,
                              k.reshape(db, 1, a), v.reshape(db, 1, a), kt_cache, vt_cache, n_pages)
        prev = state_conv[l]
        x2 = _merge(hs, z, prev[:, 1], prev[:, 0], conv_zero, bc, sgc, sga, o.reshape(db, a), cw,
                    woc, woa, wout, 1, False)
        xs = _ffn(x2, g2, w2_in, w2_out, g_final, last)
        outs_s["k"].append(kt)
        outs_s["v"].append(vt)
        outs_s["ik"].append(kit)
        outs_s["conv"].append(jnp.stack([prev[:, 1], z], axis=1))

    def heads_last(parts, lead, n_pos):
        x = jnp.stack(parts).reshape(depth, -1, N_HEADS, HEAD_DIM, n_pos)
        return jnp.transpose(x, (0, 1, 4, 2, 3)).reshape((depth,) + lead + (N_HEADS, HEAD_DIM))

    def dim_last(parts, lead, n_pos):
        x = jnp.transpose(jnp.stack(parts), (0, 1, 3, 2))
        return x.reshape((depth,) + lead + (IDX_DIM,))

    return (xp.reshape(batch, seq, d),
            xs.reshape(db, dec_seq, d),
            heads_last(outs_p["k"], (batch, seq), seq),
            heads_last(outs_p["v"], (batch, seq), seq),
            dim_last(outs_p["ik"], (batch, seq), seq),
            jnp.stack(outs_p["conv"]),
            heads_last(outs_s["k"], (db, dec_seq), db),
            heads_last(outs_s["v"], (db, dec_seq), db),
            dim_last(outs_s["ik"], (db, dec_seq), db),
            jnp.stack(outs_s["conv"]))
```

```python
import functools
import math

import jax
import jax.numpy as jnp
from jax import lax
from jax.experimental import pallas as pl
from jax.experimental.pallas import tpu as pltpu

F32 = jnp.float32
BF16 = jnp.bfloat16
I32 = jnp.int32

EPS = 1e-6
NEG_INF = -1e30
HEAD_DIM = 64
N_HEADS = 8
IDX_HEADS = 8
IDX_DIM = 64
TOPK_MAX = 256
N_BUCKETS = 32
REL_MAX_DIST = 128
CONV_W = 3
LANES = 128
SUBLANES = 8
HEADS_PER_VREG = LANES // HEAD_DIM
VMEM_LIMIT_BYTES = 56 * 1024 * 1024
ROW_TILE = 512
ATT_TILE = 256
FFN_CHUNK = 256
SCORE_SAMPLES_PER_STEP = 4
INT_MAX = 2**31 - 1
LOG2E = math.log2(math.e)
I16 = jnp.int16
PACKED_ROWS = 2 * SUBLANES
HALF_BITS = 16
HALF_MASK = 2**HALF_BITS - 1
HALF_BIAS = 2**(HALF_BITS - 1)
COUNT_UNKNOWN = 2.0**30


def _cparams(n_axes):
    return pltpu.CompilerParams(dimension_semantics=("arbitrary",) * n_axes,
                                vmem_limit_bytes=VMEM_LIMIT_BYTES)


def _resident(shape, index_map):
    return pl.BlockSpec(shape, index_map, pipeline_mode=pl.Buffered(1))


def _rms(x, g):
    return x * lax.rsqrt(jnp.mean(x * x, axis=-1, keepdims=True) + EPS) * g


def _sigmoid(x):
    return 1.0 / (1.0 + jnp.exp(-x))


def _dot(a, b):
    return jnp.dot(a, b, preferred_element_type=F32)


def _dot_nt(a, b):
    return lax.dot_general(a, b, (((1,), (1,)), ((), ())), preferred_element_type=F32)


def _ffn_kernel(x_ref, g_ref, wgu_ref, wd_ref, gf_ref, o_ref, acc_ref, *, d_ff, final_norm):
    x = x_ref[...]
    xb = _rms(x, g_ref[...]).astype(BF16)
    for j in range(d_ff // FFN_CHUNK):
        lo = j * FFN_CHUNK
        a = _dot(xb, wgu_ref[:, lo:lo + FFN_CHUNK])
        b = _dot(xb, wgu_ref[:, d_ff + lo:d_ff + lo + FFN_CHUNK])
        hid = (a * _sigmoid(a) * b).astype(BF16)
        part = _dot(hid, wd_ref[lo:lo + FFN_CHUNK, :])
        if j == 0:
            acc_ref[...] = part
        else:
            acc_ref[...] += part
    y = x + 0.5 * acc_ref[...]
    if final_norm:
        y = _rms(y, gf_ref[...])
    o_ref[...] = y


def _ffn(x, g, wgu, wd, g_final, final_norm):
    m, d = x.shape
    d_ff = wd.shape[0]
    tm = min(ROW_TILE, m)
    return pl.pallas_call(
        functools.partial(_ffn_kernel, d_ff=d_ff, final_norm=final_norm),
        grid=(m // tm,),
        in_specs=[pl.BlockSpec((tm, d), lambda i: (i, 0)),
                  _resident((1, d), lambda i: (0, 0)),
                  _resident((d, 2 * d_ff), lambda i: (0, 0)),
                  _resident((d_ff, d), lambda i: (0, 0)),
                  _resident((1, d), lambda i: (0, 0))],
        out_specs=pl.BlockSpec((tm, d), lambda i: (i, 0)),
        out_shape=jax.ShapeDtypeStruct((m, d), F32),
        scratch_shapes=[pltpu.VMEM((tm, d), F32)],
        compiler_params=_cparams(1),
        name="ffn",
    )(x, g, wgu, wd, g_final)


def _proj_kernel(h_ref, g_ref, w_ref, z_ref, bc_ref, sgc_ref, sga_ref, kt_ref, vt_ref, kit_ref, *refs,
                 c, a, d, prompt):
    u = _rms(h_ref[...], g_ref[...]).astype(BF16)

    def mm(lo, hi):
        return _dot(u, w_ref[:, lo:hi])

    z_ref[...] = mm(2 * c, 3 * c) * mm(0, c)
    bc_ref[...] = mm(c, 2 * c)
    o = 3 * c
    q = mm(o, o + a)
    k = mm(o + a, o + 2 * a)
    v = mm(o + 2 * a, o + 3 * a)
    o += 3 * a
    qi = mm(o, o + IDX_HEADS * IDX_DIM) * IDX_DIM ** -0.5
    o += IDX_HEADS * IDX_DIM
    ki = mm(o, o + LANES)
    wi = mm(o + LANES, o + 2 * LANES) * IDX_HEADS ** -0.5
    o += 2 * LANES
    sgc_ref[...] = _sigmoid(mm(o, o + d))
    sga_ref[...] = _sigmoid(mm(o + d, o + 2 * d))
    kt = k.T
    vt = v.T
    kt_ref[0] = kt
    vt_ref[0] = vt
    kit_ref[0] = ki.T[:IDX_DIM, :]
    if prompt:
        kb_ref, ki2_ref, vtb_ref, qt_ref, qit_ref, wit_ref = refs
        kb_ref[...] = k.astype(BF16)
        ki2_ref[...] = (ki + pltpu.roll(ki, IDX_DIM, 1)).astype(BF16)
        _, n_sub, n_pairs, v_rows, t = vtb_ref.shape
        for j in range(n_sub):
            for pr in range(n_pairs):
                vtb_ref[0, j, pr, :LANES, :] = vt[pr * LANES:(pr + 1) * LANES, j * t:(j + 1) * t].astype(BF16)
                vtb_ref[0, j, pr, LANES:, :] = jnp.ones((v_rows - LANES, t), BF16)
        qt_ref[0] = (q * (HEAD_DIM ** -0.5 * LOG2E)).T.astype(BF16)
        qit_ref[0] = qi.T.astype(BF16)
        wit_ref[0] = wi.T[:IDX_HEADS, :]
    else:
        q_ref, qi_ref, wi_ref, k_ref, v_ref, ki_ref = refs
        q_ref[...] = (q * (HEAD_DIM ** -0.5 * LOG2E)).astype(BF16)
        qi_ref[...] = qi.astype(BF16)
        wi_ref[...] = wi[:, :IDX_HEADS]
        k_ref[...] = k
        v_ref[...] = v
        ki_ref[...] = ki[:, :IDX_DIM]


def _proj(h, g, w, batch, prompt, att_tile=None):
    m, d = h.shape
    n_cols = w.shape[1]
    c = a = d // 2
    seq = m // batch
    tm = min(ROW_TILE, seq)
    tiles_per_seq = seq // tm
    row = lambda i: (i, 0)
    fmaj = lambda i: (i // tiles_per_seq, 0, i % tiles_per_seq)
    chunked = lambda i: (i // tiles_per_seq, i % tiles_per_seq, 0, 0)

    def nat(cols, dtype):
        return jax.ShapeDtypeStruct((m, cols), dtype), pl.BlockSpec((tm, cols), row)

    def fm(rows, dtype):
        return jax.ShapeDtypeStruct((batch, rows, seq), dtype), pl.BlockSpec((1, rows, tm), fmaj)

    outs = [nat(c, F32),
            nat(c, F32),
            nat(d, F32),
            nat(d, F32),
            fm(a, F32),
            fm(a, F32),
            fm(IDX_DIM, F32)]
    if prompt:
        assert tm % att_tile == 0
        outs += [nat(a, BF16),
                 nat(LANES, BF16),
                 (jax.ShapeDtypeStruct((batch, seq // att_tile, a // LANES, LANES + PACKED_ROWS, att_tile), BF16),
                  pl.BlockSpec((1, tm // att_tile, a // LANES, LANES + PACKED_ROWS, att_tile),
                               lambda i: (i // tiles_per_seq, i % tiles_per_seq, 0, 0, 0))),
                 fm(a, BF16),
                 fm(a, BF16),
                 fm(IDX_HEADS, F32)]
    else:
        outs += [nat(a, BF16), nat(a, BF16), nat(IDX_HEADS, F32), nat(a, F32), nat(a, F32), nat(IDX_DIM, F32)]
    return pl.pallas_call(
        functools.partial(_proj_kernel, c=c, a=a, d=d, prompt=prompt),
        grid=(m // tm,),
        in_specs=[pl.BlockSpec((tm, d), row),
                  _resident((1, d), lambda i: (0, 0)),
                  _resident((d, n_cols), lambda i: (0, 0))],
        out_specs=[o[1] for o in outs],
        out_shape=[o[0] for o in outs],
        compiler_params=_cparams(1),
        name="proj",
    )(h, g, w)


def _sortable(x):
    x = jnp.where(x == 0.0, 0.0, x)
    b = lax.bitcast_convert_type(x, I32)
    return b ^ ((b >> 31) & I32(INT_MAX))


def _store_keys(keys_ref, hi_ref, lo_ref, ci, scores):
    k = _sortable(scores)
    keys_ref[ci] = k
    hi_ref[ci] = (k >> HALF_BITS).astype(I16)
    lo_ref[ci] = ((k & HALF_MASK) - HALF_BIAS).astype(I16)


def _count(keys_ref, n_chunks, pred):
    kc, nq = keys_ref.shape[1:]

    def body(ci, acc):
        hit = jnp.where(pred(keys_ref[ci], ci), 1.0, 0.0)
        return acc + jnp.sum(hit.reshape(kc // SUBLANES, SUBLANES, nq), axis=0)

    acc = lax.fori_loop(0, n_chunks, body, jnp.zeros((SUBLANES, nq), F32))
    return jnp.sum(acc, axis=0, keepdims=True)


def _count16(x_ref, n_chunks, bound, strict=False):
    kc, nq = x_ref.shape[1:]
    b16 = bound.astype(I16)

    def body(j, acc):
        parts = []
        for ci in (2 * j, 2 * j + 1):
            x = x_ref[ci]
            hit = jnp.where(x > b16 if strict else x >= b16, jnp.int16(1), jnp.int16(0))
            parts += [hit[r * PACKED_ROWS:(r + 1) * PACKED_ROWS, :] for r in range(kc // PACKED_ROWS)]
        while len(parts) > 1:
            parts = [a + b for a, b in zip(parts[::2], parts[1::2])] + parts[len(parts) & ~1:]
        return acc + parts[0]

    acc = lax.fori_loop(0, (n_chunks + 1) // 2, body, jnp.zeros((PACKED_ROWS, nq), I16))
    return jnp.sum(acc.astype(I32), axis=0, keepdims=True).astype(F32)


def _park_chunk(hi_ref, lo_ref, ci):
    hi_ref[ci] = jnp.full(hi_ref.shape[1:], -HALF_BIAS, I16)
    lo_ref[ci] = jnp.full(lo_ref.shape[1:], -HALF_BIAS, I16)


def _radix16(x_ref, n_chunks, rank):
    nq = x_ref.shape[2]

    def bit_step(it, carry):
        prefix, count = carry
        cand = prefix | lax.shift_left(I32(1), HALF_BITS - 1 - it)
        cnt = _count16(x_ref, n_chunks, cand - HALF_BIAS)
        accept = cnt >= rank
        return jnp.where(accept, cand, prefix), jnp.where(accept, cnt, count)

    init = (jnp.zeros((1, nq), I32), jnp.full((1, nq), COUNT_UNKNOWN, F32))
    return lax.fori_loop(0, HALF_BITS, bit_step, init)


def _select_threshold(keys_ref, hi_ref, lo_ref, n_chunks, top_k, tie_ref):
    _, kc, nq = keys_ref.shape
    kf = float(top_k)

    hi_bits, at_or_above_hi = _radix16(hi_ref, n_chunks, kf)
    hi = hi_bits - HALF_BIAS
    above = _count16(hi_ref, n_chunks, hi, strict=True)
    hi16 = hi.astype(I16)

    def keep_equal_high(j, carry):
        for ci in (2 * j, 2 * j + 1):
            lo_ref[ci] = jnp.where(hi_ref[ci] == hi16, lo_ref[ci], jnp.int16(-HALF_BIAS))
        return carry

    lax.fori_loop(0, (n_chunks + 1) // 2, keep_equal_high, 0)
    lo_bits, at_or_above_lo = _radix16(lo_ref, n_chunks, kf - above)
    thr = jnp.left_shift(hi, HALF_BITS) | lo_bits
    cnt_ge = jnp.where(lo_bits == 0, at_or_above_hi, above + at_or_above_lo)
    tie_ref[...] = jnp.full((1, nq), INT_MAX, I32)

    excess_ties = jnp.max(cnt_ge) > kf

    @pl.when(excess_ties)
    def _():
        need = kf - above - _count16(lo_ref, n_chunks, lo_bits - HALF_BIAS, strict=True)
        row = lax.broadcasted_iota(I32, (kc, nq), 0)
        n_bits = max(1, (keys_ref.shape[0] * kc - 1).bit_length())

        def idx_step(it, j):
            cand = j | lax.shift_left(I32(1), n_bits - 1 - it)
            cnt = _count(keys_ref, n_chunks,
                         lambda kk, ci: jnp.where(kk == thr, ci * kc + row, INT_MAX) < cand)
            return jnp.where(cnt < need, cand, j)

        tie_ref[...] = lax.fori_loop(0, n_bits, idx_step, jnp.zeros((1, nq), I32))

    return thr, excess_ties


def _selected(kk, key_index, thr, tie):
    return jnp.logical_or(kk > thr, jnp.logical_and(kk == thr, key_index <= tie))


def _head_rows(x_ref, h):
    rows = x_ref[0, h * HEAD_DIM:(h + 1) * HEAD_DIM, :]
    parts = [jnp.zeros_like(rows)] * HEADS_PER_VREG
    parts[h % HEADS_PER_VREG] = rows
    return jnp.concatenate(parts, axis=0)


def _attn_kernel(qt_ref, qit_ref, wit_ref, k_ref, vt_ref, ki2_ref, bias0_ref, bias1_ref, o_ref,
                 keys_ref, hi_ref, lo_ref, qm_ref, qim_ref, m_ref, alpha_ref, acc_ref, s_ref, p_ref,
                 tie_ref, *, top_k):
    t = qt_ref.shape[2]
    i = pl.program_id(1)
    key_row = lax.broadcasted_iota(I32, (t, t), 0)
    query_col = lax.broadcasted_iota(I32, (t, t), 1)
    causal = key_row <= query_col

    for h in range(N_HEADS):
        qm_ref[h] = _head_rows(qt_ref, h)
        qim_ref[h] = _head_rows(qit_ref, h)
    w = wit_ref[0]

    def score_chunk(ci, diagonal):
        kic = ki2_ref[0, ci]
        sc = jnp.zeros((t, t), F32)
        for h in range(IDX_HEADS):
            sc = sc + w[h:h + 1, :] * jnp.maximum(_dot(kic, qim_ref[h]), 0.0)
        if diagonal:
            sc = jnp.where(causal, sc, NEG_INF)
        _store_keys(keys_ref, hi_ref, lo_ref, ci, sc)

    def score_body(j, carry):
        score_chunk(2 * j, False)
        score_chunk(2 * j + 1, False)
        return carry

    lax.fori_loop(0, i // 2, score_body, 0)

    @pl.when(i % 2 == 1)
    def _():
        score_chunk(i - 1, False)

    score_chunk(i, True)
    _park_chunk(hi_ref, lo_ref, i + 1)

    thr, excess_ties = _select_threshold(keys_ref, hi_ref, lo_ref, i + 1, top_k, tie_ref)

    def store_masks(selected):
        def mask_chunk(ci, diagonal):
            sel = selected(keys_ref[ci], ci)
            if diagonal:
                sel = jnp.logical_and(sel, causal)
            keys_ref[ci] = lax.bitcast_convert_type(jnp.where(sel, 0.0, NEG_INF), I32)

        def mask_body(ci, carry):
            mask_chunk(ci, False)
            return carry

        lax.fori_loop(0, i, mask_body, 0)
        mask_chunk(i, True)

    @pl.when(excess_ties)
    def _():
        tie = tie_ref[...]
        store_masks(lambda kk, ci: _selected(kk, ci * t + key_row, thr, tie))

    @pl.when(jnp.logical_not(excess_ties))
    def _():
        store_masks(lambda kk, ci: kk >= thr)

    m_ref[...] = jnp.full(m_ref.shape, NEG_INF, F32)
    acc_ref[...] = jnp.zeros(acc_ref.shape, F32)

    def attend(units):
        def logits(h):
            pair = slice((h // HEADS_PER_VREG) * LANES, (h // HEADS_PER_VREG + 1) * LANES)
            top = None
            for u, (ci, mode) in enumerate(units):
                add = lax.bitcast_convert_type(keys_ref[ci], F32)
                if mode == "prev":
                    add = add + bias1_ref[h]
                elif mode == "diag":
                    add = add + bias0_ref[h]
                s = _dot(k_ref[0, ci, :, pair], qm_ref[h]) + add
                s_ref[u, h] = s
                s_max = jnp.max(s, axis=0, keepdims=True)
                top = s_max if top is None else jnp.maximum(top, s_max)
            m_old = m_ref[h]
            m_new = jnp.maximum(m_old, top)
            alpha_ref[h] = jnp.exp2(m_old - m_new)
            m_ref[h] = m_new

        def weights(h):
            for u in range(len(units)):
                p_ref[u, h] = jnp.exp2((s_ref[u, h] - m_ref[h]).astype(BF16))

        def accumulate(h):
            acc = alpha_ref[h] * acc_ref[h]
            for u, (ci, _) in enumerate(units):
                acc = acc + _dot(vt_ref[0, ci, h // HEADS_PER_VREG], p_ref[u, h])
            acc_ref[h] = acc

        for stage in (logits, weights, accumulate):
            for h in range(N_HEADS):
                stage(h)

    n_far = jnp.maximum(i - 1, 0)

    def far_body(j, carry):
        attend(((2 * j, "far"), (2 * j + 1, "far")))
        return carry

    lax.fori_loop(0, n_far // 2, far_body, 0)

    @pl.when(n_far % 2 == 1)
    def _():
        attend(((n_far - 1, "far"),))

    @pl.when(i >= 1)
    def _():
        attend(((i - 1, "prev"), (i, "diag")))

    @pl.when(i == 0)
    def _():
        attend(((i, "diag"),))

    for pr in range(N_HEADS // HEADS_PER_VREG):
        parts = []
        for hh in range(HEADS_PER_VREG):
            h = pr * HEADS_PER_VREG + hh
            parts.append(acc_ref[h, hh * HEAD_DIM:(hh + 1) * HEAD_DIM, :] / acc_ref[h, LANES:LANES + 1, :])
        o_ref[:, pr * LANES:(pr + 1) * LANES] = jnp.concatenate(parts, axis=0).T.astype(BF16)


def _prompt_attention(qt, qit, wit, kb, vtb5, ki2, bias0, bias1, top_k):
    batch, n_chunks, n_pairs, v_rows, t = vtb5.shape
    a = n_pairs * LANES
    seq = n_chunks * t
    qmap = lambda b, i: (b, 0, i)
    bmap = lambda b, i: (b, 0, 0, 0)
    cmap = lambda b, i: (0, 0, 0)
    kb4 = kb.reshape(batch, n_chunks, t, a)
    ki24 = ki2.reshape(batch, n_chunks, t, LANES)
    return pl.pallas_call(
        functools.partial(_attn_kernel, top_k=top_k),
        grid=(batch, n_chunks),
        in_specs=[pl.BlockSpec((1, a, t), qmap),
                  pl.BlockSpec((1, a, t), qmap),
                  pl.BlockSpec((1, IDX_HEADS, t), qmap),
                  _resident((1, n_chunks, t, a), bmap),
                  _resident((1, n_chunks, n_pairs, v_rows, t), lambda b, i: (b, 0, 0, 0, 0)),
                  _resident((1, n_chunks, t, LANES), bmap),
                  _resident((N_HEADS, t, t), cmap),
                  _resident((N_HEADS, t, t), cmap)],
        out_specs=pl.BlockSpec((t, a), lambda b, i: (b * n_chunks + i, 0)),
        out_shape=jax.ShapeDtypeStruct((batch * seq, a), BF16),
        scratch_shapes=[pltpu.VMEM((n_chunks, t, t), I32),
                        pltpu.VMEM((n_chunks + 1, t, t), I16),
                        pltpu.VMEM((n_chunks + 1, t, t), I16),
                        pltpu.VMEM((N_HEADS, LANES, t), BF16),
                        pltpu.VMEM((IDX_HEADS, LANES, t), BF16),
                        pltpu.VMEM((N_HEADS, 1, t), F32),
                        pltpu.VMEM((N_HEADS, 1, t), F32),
                        pltpu.VMEM((N_HEADS, v_rows, t), F32),
                        pltpu.VMEM((2, N_HEADS, t, t), F32),
                        pltpu.VMEM((2, N_HEADS, t, t), BF16),
                        pltpu.VMEM((1, t), I32)],
        compiler_params=_cparams(2),
        name="prompt_attn",
    )(qt, qit, wit, kb4, vtb5, ki24, bias0, bias1)


def _bias_of_dist(dist, rb_ref, h):
    n = jnp.maximum(dist, 0)
    max_exact = N_BUCKETS // 2
    nf = jnp.maximum(n, 1).astype(F32)
    large = max_exact + (jnp.log(nf / max_exact) / math.log(REL_MAX_DIST / max_exact)
                         * (N_BUCKETS - max_exact)).astype(I32)
    large = jnp.minimum(large, N_BUCKETS - 1)
    bucket = jnp.where(n < max_exact, n, large)
    out = jnp.zeros(dist.shape, F32)
    for nb in range(N_BUCKETS):
        out = jnp.where(bucket == nb, rb_ref[nb, h], out)
    return (out - rb_ref[N_BUCKETS - 1, h]) * LOG2E


def _bias_kernel(rb_ref, b0_ref, b1_ref, bs_ref, *, t, past):
    key_row = lax.broadcasted_iota(I32, (t, t), 0)
    query_col = lax.broadcasted_iota(I32, (t, t), 1)
    key_lane = lax.broadcasted_iota(I32, (1, bs_ref.shape[1]), 1)
    for h in range(N_HEADS):
        b0_ref[h] = _bias_of_dist(query_col - key_row, rb_ref, h)
        b1_ref[h] = _bias_of_dist(t + query_col - key_row, rb_ref, h)
        bs_ref[h:h + 1, :] = _bias_of_dist(past - key_lane, rb_ref, h)


def _bias_tables(rel_bias, t, past, n_keys_s):
    return pl.pallas_call(
        functools.partial(_bias_kernel, t=t, past=past),
        in_specs=[pl.BlockSpec(memory_space=pltpu.SMEM)],
        out_shape=[jax.ShapeDtypeStruct((N_HEADS, t, t), F32),
                   jax.ShapeDtypeStruct((N_HEADS, t, t), F32),
                   jax.ShapeDtypeStruct((N_HEADS, n_keys_s), F32)],
        compiler_params=pltpu.CompilerParams(vmem_limit_bytes=VMEM_LIMIT_BYTES),
        name="bias_tables",
    )(rel_bias)


def _page_specs(layer, n_pages, rows, page, samples_per_step=1):
    def spec(g, p):
        return pl.BlockSpec((1, 1, rows, page),
                            lambda b, pt: (layer, pt[(b * samples_per_step + g) * n_pages + p], 0, 0))
    return [spec(g, p) for g in range(samples_per_step) for p in range(n_pages)]


def _sscore_kernel(pt_ref, qi_ref, wi_ref, *refs, n_pages):
    ik_refs, o_ref = refs[:-1], refs[-1]
    page = ik_refs[0].shape[3]
    for j, ik_ref in enumerate(ik_refs):
        g, p = divmod(j, n_pages)
        dots = _dot(qi_ref[g], ik_ref[0, 0].astype(BF16))
        o_ref[g, :, p * page:(p + 1) * page] = jnp.sum(wi_ref[g] * jnp.maximum(dots, 0.0),
                                                       axis=0, keepdims=True)


def _sample_scores(layer, page_table_flat, qi8, wi8, idx_t, n_pages):
    db = qi8.shape[0]
    page = idx_t.shape[3]
    g = math.gcd(db, SCORE_SAMPLES_PER_STEP)
    per_step = lambda b, pt: (b, 0, 0)
    return pl.pallas_call(
        functools.partial(_sscore_kernel, n_pages=n_pages),
        grid_spec=pltpu.PrefetchScalarGridSpec(
            num_scalar_prefetch=1,
            grid=(db // g,),
            in_specs=[pl.BlockSpec((g, IDX_HEADS, IDX_DIM), per_step),
                      pl.BlockSpec((g, IDX_HEADS, 1), per_step)]
                     + _page_specs(layer, n_pages, IDX_DIM, page, g),
            out_specs=pl.BlockSpec((g, 1, n_pages * page), per_step)),
        out_shape=jax.ShapeDtypeStruct((db, 1, n_pages * page), F32),
        compiler_params=_cparams(1),
        name="sample_scores",
    )(page_table_flat, qi8, wi8, *([idx_t] * (g * n_pages)))


def _sselect_kernel(sc_ref, qi_ref, kin_ref, wi_ref, o_ref, keys_ref, hi_ref, lo_ref, tie_ref, *, top_k, past):
    n_chunks, page, db = keys_ref.shape
    n_pages = n_chunks - 1
    for p in range(n_pages):
        _store_keys(keys_ref, hi_ref, lo_ref, p, sc_ref[:, p * page:(p + 1) * page].T)
    kin = kin_ref[...].astype(BF16).astype(F32)
    qif = qi_ref[...].astype(F32)
    w = wi_ref[...]
    s_new = jnp.zeros((db, 1), F32)
    for h in range(IDX_HEADS):
        qh = qif[:, h * IDX_DIM:(h + 1) * IDX_DIM]
        s_new = s_new + w[:, h:h + 1] * jnp.maximum(jnp.sum(qh * kin, axis=-1, keepdims=True), 0.0)
    lane = lax.broadcasted_iota(I32, (db, page), 1)
    _store_keys(keys_ref, hi_ref, lo_ref, n_pages, jnp.where(lane == 0, s_new, NEG_INF).T)
    _park_chunk(hi_ref, lo_ref, n_chunks)

    thr, _ = _select_threshold(keys_ref, hi_ref, lo_ref, n_chunks, top_k, tie_ref)
    tie = tie_ref[...]
    row = lax.broadcasted_iota(I32, (page, db), 0)
    for p in range(n_chunks):
        key_index = p * page + row
        sel = jnp.logical_and(key_index <= past, _selected(keys_ref[p], key_index, thr, tie))
        o_ref[:, p * page:(p + 1) * page] = jnp.where(sel, 0.0, NEG_INF).T


def _sample_select(scores, qi, ki_new, wi, top_k, past, page):
    db = scores.shape[0]
    n_chunks = past // page + 1
    return pl.pallas_call(
        functools.partial(_sselect_kernel, top_k=top_k, past=past),
        out_shape=jax.ShapeDtypeStruct((db, n_chunks * page), F32),
        scratch_shapes=[pltpu.VMEM((n_chunks, page, db), I32),
                        pltpu.VMEM((n_chunks + 1, page, db), I16),
                        pltpu.VMEM((n_chunks + 1, page, db), I16),
                        pltpu.VMEM((1, db), I32)],
        compiler_params=pltpu.CompilerParams(vmem_limit_bytes=VMEM_LIMIT_BYTES),
        name="sample_select",
    )(scores, qi, ki_new, wi)


def _sattn_kernel(pt_ref, q_ref, mask_ref, bias_ref, kn_ref, vn_ref, *refs):
    n_pages = (len(refs) - 1) // 2
    k_refs, v_refs, o_ref = refs[:n_pages], refs[n_pages:2 * n_pages], refs[-1]
    a = q_ref.shape[2]
    page = k_refs[0].shape[3]
    q = q_ref[0]
    lane_head = lax.shift_right_logical(lax.broadcasted_iota(I32, (N_HEADS, a), 1),
                                        HEAD_DIM.bit_length() - 1)
    own = lane_head == lax.broadcasted_iota(I32, (N_HEADS, a), 0)
    qf = q.astype(F32)
    q_heads = jnp.where(own, qf, 0.0).astype(BF16)
    parts = [_dot(q_heads, k_ref[0, 0].astype(BF16)) for k_ref in k_refs]
    kn = kn_ref[0].astype(BF16).astype(F32)
    g_new = jnp.sum(jnp.where(own, qf * kn, 0.0), axis=-1, keepdims=True)
    parts.append(jnp.where(lax.broadcasted_iota(I32, (N_HEADS, page), 1) == 0, g_new, 0.0))
    s = jnp.concatenate(parts, axis=1) + bias_ref[...] + mask_ref[0]
    p = jnp.exp2(s - jnp.max(s, axis=-1, keepdims=True))
    l = jnp.sum(p, axis=-1, keepdims=True)
    new_col = n_pages * page
    acc = p[:, new_col:new_col + 1] * vn_ref[0].astype(BF16).astype(F32)
    for pg, v_ref in enumerate(v_refs):
        acc = acc + _dot_nt(p[:, pg * page:(pg + 1) * page].astype(BF16), v_ref[0, 0].astype(BF16))
    o_ref[0] = jnp.sum(jnp.where(own, acc / l, 0.0), axis=0, keepdims=True).astype(BF16)


def _sample_attention(layer, page_table_flat, q3, mask3, bias_s, k_new3, v_new3, kt_cache, vt_cache, n_pages):
    db, _, a = q3.shape
    page = kt_cache.shape[3]
    n_keys = mask3.shape[2]
    per_sample = lambda b, pt: (b, 0, 0)
    return pl.pallas_call(
        _sattn_kernel,
        grid_spec=pltpu.PrefetchScalarGridSpec(
            num_scalar_prefetch=1,
            grid=(db,),
            in_specs=[pl.BlockSpec((1, 1, a), per_sample),
                      pl.BlockSpec((1, 1, n_keys), per_sample),
                      _resident((N_HEADS, n_keys), lambda b, pt: (0, 0)),
                      pl.BlockSpec((1, 1, a), per_sample),
                      pl.BlockSpec((1, 1, a), per_sample)]
                     + _page_specs(layer, n_pages, a, page) + _page_specs(layer, n_pages, a, page),
            out_specs=pl.BlockSpec((1, 1, a), per_sample)),
        out_shape=jax.ShapeDtypeStruct((db, 1, a), BF16),
        compiler_params=_cparams(1),
        name="sample_attn",
    )(page_table_flat, q3, mask3, bias_s, k_new3, v_new3, *([kt_cache] * n_pages), *([vt_cache] * n_pages))


def _merge_kernel(h_ref, z_ref, zp1_ref, zp2_ref, prev_ref, bc_ref, sgc_ref, sga_ref, o_ref, cw_ref,
                  woc_ref, woa_ref, wout_ref, x_ref, *, tiles_per_seq, sequence):
    z = z_ref[...]
    if sequence:
        first = (pl.program_id(0) % tiles_per_seq) == 0
        before1 = jnp.where(first, prev_ref[0, 1:2, :], zp1_ref[7:8, :])
        before2 = jnp.where(first, prev_ref[0, 0:1, :], zp1_ref[6:7, :])
        rowi = lax.broadcasted_iota(I32, z.shape, 0)
        z1 = jnp.where(rowi == 0, before1, pltpu.roll(z, 1, 0))
        z2 = jnp.where(rowi == 0, before2, jnp.where(rowi == 1, before1, pltpu.roll(z, 2, 0)))
    else:
        z1 = zp1_ref[...]
        z2 = zp2_ref[...]
    conv = cw_ref[0:1, :] * z2 + cw_ref[1:2, :] * z1 + cw_ref[2:3, :] * z
    y_conv = _dot((bc_ref[...] * conv).astype(BF16), woc_ref[...])
    y_attn = _dot(o_ref[...], woa_ref[...])
    merged = sgc_ref[...] * y_conv + sga_ref[...] * y_attn
    x_ref[...] = h_ref[...] + _dot(merged.astype(BF16), wout_ref[...])


def _merge(h, z, zp1, zp2, prev, bc, sgc, sga, o, cw, woc, woa, wout, batch, sequence):
    m, d = h.shape
    c = z.shape[1]
    seq = m // batch
    tm = min(ROW_TILE, seq) if sequence else min(ROW_TILE, m)
    tiles_per_seq = seq // tm if sequence else 1
    row = lambda i: (i, 0)
    const = lambda i: (0, 0)
    if sequence:
        halo = tm // SUBLANES
        zp1_spec = pl.BlockSpec((SUBLANES, c), lambda i: (jnp.maximum(i * halo - 1, 0), 0))
        zp2_spec = pl.BlockSpec((SUBLANES, c), lambda i: (0, 0))
        prev_spec = pl.BlockSpec((1, CONV_W - 1, c), lambda i: (i // tiles_per_seq, 0, 0))
    else:
        zp1_spec = pl.BlockSpec((tm, c), row)
        zp2_spec = pl.BlockSpec((tm, c), row)
        prev_spec = pl.BlockSpec((1, CONV_W - 1, c), lambda i: (0, 0, 0))
    return pl.pallas_call(
        functools.partial(_merge_kernel, tiles_per_seq=tiles_per_seq, sequence=sequence),
        grid=(m // tm,),
        in_specs=[pl.BlockSpec((tm, d), row),
                  pl.BlockSpec((tm, c), row),
                  zp1_spec, zp2_spec, prev_spec,
                  pl.BlockSpec((tm, c), row),
                  pl.BlockSpec((tm, d), row),
                  pl.BlockSpec((tm, d), row),
                  pl.BlockSpec((tm, o.shape[1]), row),
                  _resident((CONV_W, c), const),
                  _resident(woc.shape, const),
                  _resident(woa.shape, const),
                  _resident(wout.shape, const)],
        out_specs=pl.BlockSpec((tm, d), row),
        out_shape=jax.ShapeDtypeStruct((m, d), F32),
        compiler_params=_cparams(1),
        name="merge",
    )(h, z, zp1, zp2, prev, bc, sgc, sga, o, cw, woc, woa, wout)


def _arrange_w_in(w, d):
    c = a = d // 2
    o = 3 * c + 3 * a + IDX_HEADS * IDX_DIM
    head = w[:, :o]
    ki = jnp.pad(w[:, o:o + IDX_DIM], ((0, 0), (0, LANES - IDX_DIM)))
    wi = jnp.pad(w[:, o + IDX_DIM:o + IDX_DIM + IDX_HEADS], ((0, 0), (0, LANES - IDX_HEADS)))
    gates = w[:, o + IDX_DIM + IDX_HEADS:]
    return jnp.concatenate([head, ki, wi, gates], axis=1).astype(BF16)


def kernel(x_prompt, x_sample, cache_k, cache_v, cache_idx_k, state_conv, page_table, rel_bias,
           norm_ffn1, w_ffn1_in, w_ffn1_out, norm_mix, w_in, conv_w, w_o_conv, w_o_attn, w_out,
           norm_ffn2, w_ffn2_in, w_ffn2_out, norm_final):
    batch, seq, d = x_prompt.shape
    db, dec_seq, _ = x_sample.shape
    assert dec_seq == 1
    depth = w_in.shape[0]
    c = a = d // 2
    n_pool, page = cache_k.shape[1:3]
    n_pages = page_table.shape[1]
    past = n_pages * page
    n_keys_s = past + page
    t = min(ATT_TILE, seq)
    assert seq % t == 0 and seq % min(ROW_TILE, seq) == 0 and t >= REL_MAX_DIST
    top_k_p = min(TOPK_MAX, seq // 4)
    top_k_s = min(TOPK_MAX, (past + dec_seq) // 4)

    bias0, bias1, bias_s = _bias_tables(rel_bias, t, past, n_keys_s)
    pt_flat = page_table.reshape(-1)
    g_final = norm_final.reshape(1, d)
    conv_zero = jnp.zeros((batch, CONV_W - 1, c), F32)
    kt_cache = jnp.transpose(cache_k, (0, 1, 3, 4, 2)).reshape(depth, n_pool, a, page)
    vt_cache = jnp.transpose(cache_v, (0, 1, 3, 4, 2)).reshape(depth, n_pool, a, page)
    idx_t = jnp.transpose(cache_idx_k, (0, 1, 3, 2))

    xp = x_prompt.reshape(batch * seq, d)
    xs = x_sample.reshape(db, d)
    outs_p = {"k": [], "v": [], "ik": [], "conv": []}
    outs_s = {"k": [], "v": [], "ik": [], "conv": []}

    for l in range(depth):
        last = l == depth - 1
        g1 = norm_ffn1[l].reshape(1, d)
        g2 = norm_ffn2[l].reshape(1, d)
        gm = norm_mix[l].reshape(1, d)
        w1_in, w1_out = w_ffn1_in[l].astype(BF16), w_ffn1_out[l].astype(BF16)
        w2_in, w2_out = w_ffn2_in[l].astype(BF16), w_ffn2_out[l].astype(BF16)
        wp = _arrange_w_in(w_in[l], d)
        woc, woa, wout = w_o_conv[l].astype(BF16), w_o_attn[l].astype(BF16), w_out[l].astype(BF16)
        cw = conv_w[l]

        hp = _ffn(xp, g1, w1_in, w1_out, g_final, False)
        (z, bc, sgc, sga, kt, vt, kit, kb, ki2, vtb, qt, qit, wit) = _proj(hp, gm, wp, batch, True, t)
        o = _prompt_attention(qt, qit, wit, kb, vtb, ki2, bias0, bias1, top_k_p)
        x2 = _merge(hp, z, z, z, conv_zero, bc, sgc, sga, o, cw, woc, woa, wout, batch, True)
        xp = _ffn(x2, g2, w2_in, w2_out, g_final, last)
        outs_p["k"].append(kt)
        outs_p["v"].append(vt)
        outs_p["ik"].append(kit)
        outs_p["conv"].append(z.reshape(batch, seq, c)[:, seq - (CONV_W - 1):])

        hs = _ffn(xs, g1, w1_in, w1_out, g_final, False)
        (z, bc, sgc, sga, kt, vt, kit, q, qi, wi, k, v, ki) = _proj(hs, gm, wp, 1, False)
        scores = _sample_scores(l, pt_flat, qi.reshape(db, IDX_HEADS, IDX_DIM),
                                wi.reshape(db, IDX_HEADS, 1), idx_t, n_pages)
        mask = _sample_select(scores.reshape(db, past), qi, ki, wi, top_k_s, past, page)
        o = _sample_attention(l, pt_flat, q.reshape(db, 1, a), mask.reshape(db, 1, n_keys_s), bias_s,
                              k.reshape(db, 1, a), v.reshape(db, 1, a), kt_cache, vt_cache, n_pages)
        prev = state_conv[l]
        x2 = _merge(hs, z, prev[:, 1], prev[:, 0], conv_zero, bc, sgc, sga, o.reshape(db, a), cw,
                    woc, woa, wout, 1, False)
        xs = _ffn(x2, g2, w2_in, w2_out, g_final, last)
        outs_s["k"].append(kt)
        outs_s["v"].append(vt)
        outs_s["ik"].append(kit)
        outs_s["conv"].append(jnp.stack([prev[:, 1], z], axis=1))

    def heads_last(parts, lead, n_pos):
        x = jnp.stack(parts).reshape(depth, -1, N_HEADS, HEAD_DIM, n_pos)
        return jnp.transpose(x, (0, 1, 4, 2, 3)).reshape((depth,) + lead + (N_HEADS, HEAD_DIM))

    def dim_last(parts, lead, n_pos):
        x = jnp.transpose(jnp.stack(parts), (0, 1, 3, 2))
        return x.reshape((depth,) + lead + (IDX_DIM,))

    return (xp.reshape(batch, seq, d),
            xs.reshape(db, dec_seq, d),
            heads_last(outs_p["k"], (batch, seq), seq),
            heads_last(outs_p["v"], (batch, seq), seq),
            dim_last(outs_p["ik"], (batch, seq), seq),
            jnp.stack(outs_p["conv"]),
            heads_last(outs_s["k"], (db, dec_seq), db),
            heads_last(outs_s["v"], (db, dec_seq), db),
            dim_last(outs_s["ik"], (db, dec_seq), db),
            jnp.stack(outs_s["conv"]))
```

```python
import functools
import math

import jax
import jax.numpy as jnp
from jax import lax
from jax.experimental import pallas as pl
from jax.experimental.pallas import tpu as pltpu

F32 = jnp.float32
BF16 = jnp.bfloat16
I32 = jnp.int32

EPS = 1e-6
NEG_INF = -1e30
HEAD_DIM = 64
N_HEADS = 8
IDX_HEADS = 8
IDX_DIM = 64
TOPK_MAX = 256
N_BUCKETS = 32
REL_MAX_DIST = 128
CONV_W = 3
LANES = 128
SUBLANES = 8
HEADS_PER_VREG = LANES // HEAD_DIM
VMEM_LIMIT_BYTES = 56 * 1024 * 1024
ROW_TILE = 512
ATT_TILE = 256
FFN_CHUNK = 256
SCORE_SAMPLES_PER_STEP = 4
INT_MAX = 2**31 - 1
LOG2E = math.log2(math.e)
I16 = jnp.int16
PACKED_ROWS = 2 * SUBLANES
HALF_BITS = 16
HALF_MASK = 2**HALF_BITS - 1
HALF_BIAS = 2**(HALF_BITS - 1)
COUNT_UNKNOWN = 2.0**30


def _cparams(n_axes):
    return pltpu.CompilerParams(dimension_semantics=("arbitrary",) * n_axes,
                                vmem_limit_bytes=VMEM_LIMIT_BYTES)


def _resident(shape, index_map):
    return pl.BlockSpec(shape, index_map, pipeline_mode=pl.Buffered(1))


def _rms(x, g):
    return x * lax.rsqrt(jnp.mean(x * x, axis=-1, keepdims=True) + EPS) * g


def _sigmoid(x):
    return 1.0 / (1.0 + jnp.exp(-x))


def _dot(a, b):
    return jnp.dot(a, b, preferred_element_type=F32)


def _dot_nt(a, b):
    return lax.dot_general(a, b, (((1,), (1,)), ((), ())), preferred_element_type=F32)


def _ffn_rows(x, g_ref, wgu_ref, wd_ref, gf_ref, acc_ref, d_ff, final_norm):
    xb = _rms(x, g_ref[...]).astype(BF16)
    for j in range(d_ff // FFN_CHUNK):
        lo = j * FFN_CHUNK
        a = _dot(xb, wgu_ref[:, lo:lo + FFN_CHUNK])
        b = _dot(xb, wgu_ref[:, d_ff + lo:d_ff + lo + FFN_CHUNK])
        hid = (a * _sigmoid(a) * b).astype(BF16)
        part = _dot(hid, wd_ref[lo:lo + FFN_CHUNK, :])
        if j == 0:
            acc_ref[...] = part
        else:
            acc_ref[...] += part
    y = x + 0.5 * acc_ref[...]
    if final_norm:
        y = _rms(y, gf_ref[...])
    return y


def _ffn_kernel(x_ref, g_ref, wgu_ref, wd_ref, gf_ref, o_ref, acc_ref, *, d_ff, final_norm):
    o_ref[...] = _ffn_rows(x_ref[...], g_ref, wgu_ref, wd_ref, gf_ref, acc_ref, d_ff, final_norm)


def _ffn(x, g, wgu, wd, g_final, final_norm):
    m, d = x.shape
    d_ff = wd.shape[0]
    tm = min(ROW_TILE, m)
    return pl.pallas_call(
        functools.partial(_ffn_kernel, d_ff=d_ff, final_norm=final_norm),
        grid=(m // tm,),
        in_specs=[pl.BlockSpec((tm, d), lambda i: (i, 0)),
                  _resident((1, d), lambda i: (0, 0)),
                  _resident((d, 2 * d_ff), lambda i: (0, 0)),
                  _resident((d_ff, d), lambda i: (0, 0)),
                  _resident((1, d), lambda i: (0, 0))],
        out_specs=pl.BlockSpec((tm, d), lambda i: (i, 0)),
        out_shape=jax.ShapeDtypeStruct((m, d), F32),
        scratch_shapes=[pltpu.VMEM((tm, d), F32)],
        compiler_params=_cparams(1),
        name="ffn",
    )(x, g, wgu, wd, g_final)


def _proj_kernel(h_ref, g_ref, w_ref, z_ref, bc_ref, sgc_ref, sga_ref, kt_ref, vt_ref, kit_ref, *refs,
                 c, a, d, prompt):
    u = _rms(h_ref[...], g_ref[...]).astype(BF16)

    def mm(lo, hi):
        return _dot(u, w_ref[:, lo:hi])

    z_ref[...] = mm(2 * c, 3 * c) * mm(0, c)
    bc_ref[...] = mm(c, 2 * c)
    o = 3 * c
    q = mm(o, o + a)
    k = mm(o + a, o + 2 * a)
    v = mm(o + 2 * a, o + 3 * a)
    o += 3 * a
    qi = mm(o, o + IDX_HEADS * IDX_DIM) * IDX_DIM ** -0.5
    o += IDX_HEADS * IDX_DIM
    ki = mm(o, o + LANES)
    wi = mm(o + LANES, o + 2 * LANES) * IDX_HEADS ** -0.5
    o += 2 * LANES
    sgc_ref[...] = _sigmoid(mm(o, o + d))
    sga_ref[...] = _sigmoid(mm(o + d, o + 2 * d))
    kt = k.T
    vt = v.T
    kt_ref[0] = kt
    vt_ref[0] = vt
    kit_ref[0] = ki.T[:IDX_DIM, :]
    if prompt:
        kb_ref, ki2_ref, vtb_ref, qt_ref, qit_ref, wit_ref = refs
        kb_ref[...] = k.astype(BF16)
        ki2_ref[...] = (ki + pltpu.roll(ki, IDX_DIM, 1)).astype(BF16)
        _, n_sub, n_pairs, v_rows, t = vtb_ref.shape
        for j in range(n_sub):
            for pr in range(n_pairs):
                vtb_ref[0, j, pr, :LANES, :] = vt[pr * LANES:(pr + 1) * LANES, j * t:(j + 1) * t].astype(BF16)
                vtb_ref[0, j, pr, LANES:, :] = jnp.ones((v_rows - LANES, t), BF16)
        qt_ref[0] = (q * (HEAD_DIM ** -0.5 * LOG2E)).T.astype(BF16)
        qit_ref[0] = qi.T.astype(BF16)
        wit_ref[0] = wi.T[:IDX_HEADS, :]
    else:
        q_ref, qi_ref, wi_ref, k_ref, v_ref, ki_ref = refs
        q_ref[...] = (q * (HEAD_DIM ** -0.5 * LOG2E)).astype(BF16)
        qi_ref[...] = qi.astype(BF16)
        wi_ref[...] = wi[:, :IDX_HEADS]
        k_ref[...] = k
        v_ref[...] = v
        ki_ref[...] = ki[:, :IDX_DIM]


def _proj(h, g, w, batch, prompt, att_tile=None):
    m, d = h.shape
    n_cols = w.shape[1]
    c = a = d // 2
    seq = m // batch
    tm = min(ROW_TILE, seq)
    tiles_per_seq = seq // tm
    row = lambda i: (i, 0)
    fmaj = lambda i: (i // tiles_per_seq, 0, i % tiles_per_seq)
    chunked = lambda i: (i // tiles_per_seq, i % tiles_per_seq, 0, 0)

    def nat(cols, dtype):
        return jax.ShapeDtypeStruct((m, cols), dtype), pl.BlockSpec((tm, cols), row)

    def fm(rows, dtype):
        return jax.ShapeDtypeStruct((batch, rows, seq), dtype), pl.BlockSpec((1, rows, tm), fmaj)

    outs = [nat(c, F32),
            nat(c, F32),
            nat(d, F32),
            nat(d, F32),
            fm(a, F32),
            fm(a, F32),
            fm(IDX_DIM, F32)]
    if prompt:
        assert tm % att_tile == 0
        outs += [nat(a, BF16),
                 nat(LANES, BF16),
                 (jax.ShapeDtypeStruct((batch, seq // att_tile, a // LANES, LANES + PACKED_ROWS, att_tile), BF16),
                  pl.BlockSpec((1, tm // att_tile, a // LANES, LANES + PACKED_ROWS, att_tile),
                               lambda i: (i // tiles_per_seq, i % tiles_per_seq, 0, 0, 0))),
                 fm(a, BF16),
                 fm(a, BF16),
                 fm(IDX_HEADS, F32)]
    else:
        outs += [nat(a, BF16), nat(a, BF16), nat(IDX_HEADS, F32), nat(a, F32), nat(a, F32), nat(IDX_DIM, F32)]
    return pl.pallas_call(
        functools.partial(_proj_kernel, c=c, a=a, d=d, prompt=prompt),
        grid=(m // tm,),
        in_specs=[pl.BlockSpec((tm, d), row),
                  _resident((1, d), lambda i: (0, 0)),
                  _resident((d, n_cols), lambda i: (0, 0))],
        out_specs=[o[1] for o in outs],
        out_shape=[o[0] for o in outs],
        compiler_params=_cparams(1),
        name="proj",
    )(h, g, w)


def _sortable(x):
    x = jnp.where(x == 0.0, 0.0, x)
    b = lax.bitcast_convert_type(x, I32)
    return b ^ ((b >> 31) & I32(INT_MAX))


def _store_keys(keys_ref, hi_ref, lo_ref, ci, scores):
    k = _sortable(scores)
    keys_ref[ci] = k
    hi_ref[ci] = (k >> HALF_BITS).astype(I16)
    lo_ref[ci] = ((k & HALF_MASK) - HALF_BIAS).astype(I16)


def _count(keys_ref, n_chunks, pred):
    kc, nq = keys_ref.shape[1:]

    def body(ci, acc):
        hit = jnp.where(pred(keys_ref[ci], ci), 1.0, 0.0)
        return acc + jnp.sum(hit.reshape(kc // SUBLANES, SUBLANES, nq), axis=0)

    acc = lax.fori_loop(0, n_chunks, body, jnp.zeros((SUBLANES, nq), F32))
    return jnp.sum(acc, axis=0, keepdims=True)


def _count16(x_ref, n_chunks, bound, strict=False):
    kc, nq = x_ref.shape[1:]
    b16 = bound.astype(I16)

    def body(j, acc):
        parts = []
        for ci in (2 * j, 2 * j + 1):
            x = x_ref[ci]
            hit = jnp.where(x > b16 if strict else x >= b16, jnp.int16(1), jnp.int16(0))
            parts += [hit[r * PACKED_ROWS:(r + 1) * PACKED_ROWS, :] for r in range(kc // PACKED_ROWS)]
        while len(parts) > 1:
            parts = [a + b for a, b in zip(parts[::2], parts[1::2])] + parts[len(parts) & ~1:]
        return acc + parts[0]

    acc = lax.fori_loop(0, (n_chunks + 1) // 2, body, jnp.zeros((PACKED_ROWS, nq), I16))
    return jnp.sum(acc.astype(I32), axis=0, keepdims=True).astype(F32)


def _park_chunk(hi_ref, lo_ref, ci):
    hi_ref[ci] = jnp.full(hi_ref.shape[1:], -HALF_BIAS, I16)
    lo_ref[ci] = jnp.full(lo_ref.shape[1:], -HALF_BIAS, I16)


def _radix16(x_ref, n_chunks, rank):
    nq = x_ref.shape[2]

    def bit_step(it, carry):
        prefix, count = carry
        cand = prefix | lax.shift_left(I32(1), HALF_BITS - 1 - it)
        cnt = _count16(x_ref, n_chunks, cand - HALF_BIAS)
        accept = cnt >= rank
        return jnp.where(accept, cand, prefix), jnp.where(accept, cnt, count)

    init = (jnp.zeros((1, nq), I32), jnp.full((1, nq), COUNT_UNKNOWN, F32))
    return lax.fori_loop(0, HALF_BITS, bit_step, init)


def _select_threshold(keys_ref, hi_ref, lo_ref, n_chunks, top_k, tie_ref):
    _, kc, nq = keys_ref.shape
    kf = float(top_k)

    hi_bits, at_or_above_hi = _radix16(hi_ref, n_chunks, kf)
    hi = hi_bits - HALF_BIAS
    above = _count16(hi_ref, n_chunks, hi, strict=True)
    hi16 = hi.astype(I16)

    def keep_equal_high(j, carry):
        for ci in (2 * j, 2 * j + 1):
            lo_ref[ci] = jnp.where(hi_ref[ci] == hi16, lo_ref[ci], jnp.int16(-HALF_BIAS))
        return carry

    lax.fori_loop(0, (n_chunks + 1) // 2, keep_equal_high, 0)
    lo_bits, at_or_above_lo = _radix16(lo_ref, n_chunks, kf - above)
    thr = jnp.left_shift(hi, HALF_BITS) | lo_bits
    cnt_ge = jnp.where(lo_bits == 0, at_or_above_hi, above + at_or_above_lo)
    tie_ref[...] = jnp.full((1, nq), INT_MAX, I32)

    excess_ties = jnp.max(cnt_ge) > kf

    @pl.when(excess_ties)
    def _():
        need = kf - above - _count16(lo_ref, n_chunks, lo_bits - HALF_BIAS, strict=True)
        row = lax.broadcasted_iota(I32, (kc, nq), 0)
        n_bits = max(1, (keys_ref.shape[0] * kc - 1).bit_length())

        def idx_step(it, j):
            cand = j | lax.shift_left(I32(1), n_bits - 1 - it)
            cnt = _count(keys_ref, n_chunks,
                         lambda kk, ci: jnp.where(kk == thr, ci * kc + row, INT_MAX) < cand)
            return jnp.where(cnt < need, cand, j)

        tie_ref[...] = lax.fori_loop(0, n_bits, idx_step, jnp.zeros((1, nq), I32))

    return thr, excess_ties


def _selected(kk, key_index, thr, tie):
    return jnp.logical_or(kk > thr, jnp.logical_and(kk == thr, key_index <= tie))


def _head_rows(x_ref, h):
    rows = x_ref[0, h * HEAD_DIM:(h + 1) * HEAD_DIM, :]
    parts = [jnp.zeros_like(rows)] * HEADS_PER_VREG
    parts[h % HEADS_PER_VREG] = rows
    return jnp.concatenate(parts, axis=0)


def _attn_kernel(qt_ref, qit_ref, wit_ref, k_ref, vt_ref, ki2_ref, bias0_ref, bias1_ref, o_ref,
                 keys_ref, hi_ref, lo_ref, qm_ref, qim_ref, m_ref, alpha_ref, acc_ref, s_ref, p_ref,
                 tie_ref, *, top_k):
    t = qt_ref.shape[2]
    i = pl.program_id(1)
    key_row = lax.broadcasted_iota(I32, (t, t), 0)
    query_col = lax.broadcasted_iota(I32, (t, t), 1)
    causal = key_row <= query_col

    for h in range(N_HEADS):
        qm_ref[h] = _head_rows(qt_ref, h)
        qim_ref[h] = _head_rows(qit_ref, h)
    w = wit_ref[0]

    def score_chunk(ci, diagonal):
        kic = ki2_ref[0, ci]
        sc = jnp.zeros((t, t), F32)
        for h in range(IDX_HEADS):
            sc = sc + w[h:h + 1, :] * jnp.maximum(_dot(kic, qim_ref[h]), 0.0)
        if diagonal:
            sc = jnp.where(causal, sc, NEG_INF)
        _store_keys(keys_ref, hi_ref, lo_ref, ci, sc)

    def score_body(j, carry):
        score_chunk(2 * j, False)
        score_chunk(2 * j + 1, False)
        return carry

    lax.fori_loop(0, i // 2, score_body, 0)

    @pl.when(i % 2 == 1)
    def _():
        score_chunk(i - 1, False)

    score_chunk(i, True)
    _park_chunk(hi_ref, lo_ref, i + 1)

    thr, excess_ties = _select_threshold(keys_ref, hi_ref, lo_ref, i + 1, top_k, tie_ref)

    def store_masks(selected):
        def mask_chunk(ci, diagonal):
            sel = selected(keys_ref[ci], ci)
            if diagonal:
                sel = jnp.logical_and(sel, causal)
            keys_ref[ci] = lax.bitcast_convert_type(jnp.where(sel, 0.0, NEG_INF), I32)

        def mask_body(ci, carry):
            mask_chunk(ci, False)
            return carry

        lax.fori_loop(0, i, mask_body, 0)
        mask_chunk(i, True)

    @pl.when(excess_ties)
    def _():
        tie = tie_ref[...]
        store_masks(lambda kk, ci: _selected(kk, ci * t + key_row, thr, tie))

    @pl.when(jnp.logical_not(excess_ties))
    def _():
        store_masks(lambda kk, ci: kk >= thr)

    m_ref[...] = jnp.full(m_ref.shape, NEG_INF, F32)
    acc_ref[...] = jnp.zeros(acc_ref.shape, F32)

    def attend(units):
        def logits(h):
            pair = slice((h // HEADS_PER_VREG) * LANES, (h // HEADS_PER_VREG + 1) * LANES)
            top = None
            for u, (ci, mode) in enumerate(units):
                add = lax.bitcast_convert_type(keys_ref[ci], F32)
                if mode == "prev":
                    add = add + bias1_ref[h]
                elif mode == "diag":
                    add = add + bias0_ref[h]
                s = _dot(k_ref[0, ci, :, pair], qm_ref[h]) + add
                s_ref[u, h] = s
                s_max = jnp.max(s, axis=0, keepdims=True)
                top = s_max if top is None else jnp.maximum(top, s_max)
            m_old = m_ref[h]
            m_new = jnp.maximum(m_old, top)
            alpha_ref[h] = jnp.exp2(m_old - m_new)
            m_ref[h] = m_new

        def weights(h):
            for u in range(len(units)):
                p_ref[u, h] = jnp.exp2((s_ref[u, h] - m_ref[h]).astype(BF16))

        def accumulate(h):
            acc = alpha_ref[h] * acc_ref[h]
            for u, (ci, _) in enumerate(units):
                acc = acc + _dot(vt_ref[0, ci, h // HEADS_PER_VREG], p_ref[u, h])
            acc_ref[h] = acc

        for stage in (logits, weights, accumulate):
            for h in range(N_HEADS):
                stage(h)

    n_far = jnp.maximum(i - 1, 0)

    def far_body(j, carry):
        attend(((2 * j, "far"), (2 * j + 1, "far")))
        return carry

    lax.fori_loop(0, n_far // 2, far_body, 0)

    @pl.when(n_far % 2 == 1)
    def _():
        attend(((n_far - 1, "far"),))

    @pl.when(i >= 1)
    def _():
        attend(((i - 1, "prev"), (i, "diag")))

    @pl.when(i == 0)
    def _():
        attend(((i, "diag"),))

    for pr in range(N_HEADS // HEADS_PER_VREG):
        parts = []
        for hh in range(HEADS_PER_VREG):
            h = pr * HEADS_PER_VREG + hh
            parts.append(acc_ref[h, hh * HEAD_DIM:(hh + 1) * HEAD_DIM, :] / acc_ref[h, LANES:LANES + 1, :])
        o_ref[:, pr * LANES:(pr + 1) * LANES] = jnp.concatenate(parts, axis=0).T.astype(BF16)


def _prompt_attention(qt, qit, wit, kb, vtb5, ki2, bias0, bias1, top_k):
    batch, n_chunks, n_pairs, v_rows, t = vtb5.shape
    a = n_pairs * LANES
    seq = n_chunks * t
    qmap = lambda b, i: (b, 0, i)
    bmap = lambda b, i: (b, 0, 0, 0)
    cmap = lambda b, i: (0, 0, 0)
    kb4 = kb.reshape(batch, n_chunks, t, a)
    ki24 = ki2.reshape(batch, n_chunks, t, LANES)
    return pl.pallas_call(
        functools.partial(_attn_kernel, top_k=top_k),
        grid=(batch, n_chunks),
        in_specs=[pl.BlockSpec((1, a, t), qmap),
                  pl.BlockSpec((1, a, t), qmap),
                  pl.BlockSpec((1, IDX_HEADS, t), qmap),
                  _resident((1, n_chunks, t, a), bmap),
                  _resident((1, n_chunks, n_pairs, v_rows, t), lambda b, i: (b, 0, 0, 0, 0)),
                  _resident((1, n_chunks, t, LANES), bmap),
                  _resident((N_HEADS, t, t), cmap),
                  _resident((N_HEADS, t, t), cmap)],
        out_specs=pl.BlockSpec((t, a), lambda b, i: (b * n_chunks + i, 0)),
        out_shape=jax.ShapeDtypeStruct((batch * seq, a), BF16),
        scratch_shapes=[pltpu.VMEM((n_chunks, t, t), I32),
                        pltpu.VMEM((n_chunks + 1, t, t), I16),
                        pltpu.VMEM((n_chunks + 1, t, t), I16),
                        pltpu.VMEM((N_HEADS, LANES, t), BF16),
                        pltpu.VMEM((IDX_HEADS, LANES, t), BF16),
                        pltpu.VMEM((N_HEADS, 1, t), F32),
                        pltpu.VMEM((N_HEADS, 1, t), F32),
                        pltpu.VMEM((N_HEADS, v_rows, t), F32),
                        pltpu.VMEM((2, N_HEADS, t, t), F32),
                        pltpu.VMEM((2, N_HEADS, t, t), BF16),
                        pltpu.VMEM((1, t), I32)],
        compiler_params=_cparams(2),
        name="prompt_attn",
    )(qt, qit, wit, kb4, vtb5, ki24, bias0, bias1)


def _bias_of_dist(dist, rb_ref, h):
    n = jnp.maximum(dist, 0)
    max_exact = N_BUCKETS // 2
    nf = jnp.maximum(n, 1).astype(F32)
    large = max_exact + (jnp.log(nf / max_exact) / math.log(REL_MAX_DIST / max_exact)
                         * (N_BUCKETS - max_exact)).astype(I32)
    large = jnp.minimum(large, N_BUCKETS - 1)
    bucket = jnp.where(n < max_exact, n, large)
    out = jnp.zeros(dist.shape, F32)
    for nb in range(N_BUCKETS):
        out = jnp.where(bucket == nb, rb_ref[nb, h], out)
    return (out - rb_ref[N_BUCKETS - 1, h]) * LOG2E


def _bias_kernel(rb_ref, b0_ref, b1_ref, bs_ref, *, t, past):
    key_row = lax.broadcasted_iota(I32, (t, t), 0)
    query_col = lax.broadcasted_iota(I32, (t, t), 1)
    key_lane = lax.broadcasted_iota(I32, (1, bs_ref.shape[1]), 1)
    for h in range(N_HEADS):
        b0_ref[h] = _bias_of_dist(query_col - key_row, rb_ref, h)
        b1_ref[h] = _bias_of_dist(t + query_col - key_row, rb_ref, h)
        bs_ref[h:h + 1, :] = _bias_of_dist(past - key_lane, rb_ref, h)


def _bias_tables(rel_bias, t, past, n_keys_s):
    return pl.pallas_call(
        functools.partial(_bias_kernel, t=t, past=past),
        in_specs=[pl.BlockSpec(memory_space=pltpu.SMEM)],
        out_shape=[jax.ShapeDtypeStruct((N_HEADS, t, t), F32),
                   jax.ShapeDtypeStruct((N_HEADS, t, t), F32),
                   jax.ShapeDtypeStruct((N_HEADS, n_keys_s), F32)],
        compiler_params=pltpu.CompilerParams(vmem_limit_bytes=VMEM_LIMIT_BYTES),
        name="bias_tables",
    )(rel_bias)


def _page_specs(layer, n_pages, rows, page, samples_per_step=1):
    def spec(g, p):
        return pl.BlockSpec((1, 1, rows, page),
                            lambda b, pt: (layer, pt[(b * samples_per_step + g) * n_pages + p], 0, 0))
    return [spec(g, p) for g in range(samples_per_step) for p in range(n_pages)]


def _sscore_kernel(pt_ref, qi_ref, wi_ref, *refs, n_pages):
    ik_refs, o_ref = refs[:-1], refs[-1]
    page = ik_refs[0].shape[3]
    for j, ik_ref in enumerate(ik_refs):
        g, p = divmod(j, n_pages)
        dots = _dot(qi_ref[g], ik_ref[0, 0].astype(BF16))
        o_ref[g, :, p * page:(p + 1) * page] = jnp.sum(wi_ref[g] * jnp.maximum(dots, 0.0),
                                                       axis=0, keepdims=True)


def _sample_scores(layer, page_table_flat, qi8, wi8, idx_t, n_pages):
    db = qi8.shape[0]
    page = idx_t.shape[3]
    g = math.gcd(db, SCORE_SAMPLES_PER_STEP)
    per_step = lambda b, pt: (b, 0, 0)
    return pl.pallas_call(
        functools.partial(_sscore_kernel, n_pages=n_pages),
        grid_spec=pltpu.PrefetchScalarGridSpec(
            num_scalar_prefetch=1,
            grid=(db // g,),
            in_specs=[pl.BlockSpec((g, IDX_HEADS, IDX_DIM), per_step),
                      pl.BlockSpec((g, IDX_HEADS, 1), per_step)]
                     + _page_specs(layer, n_pages, IDX_DIM, page, g),
            out_specs=pl.BlockSpec((g, 1, n_pages * page), per_step)),
        out_shape=jax.ShapeDtypeStruct((db, 1, n_pages * page), F32),
        compiler_params=_cparams(1),
        name="sample_scores",
    )(page_table_flat, qi8, wi8, *([idx_t] * (g * n_pages)))


def _sselect_kernel(sc_ref, qi_ref, kin_ref, wi_ref, o_ref, keys_ref, hi_ref, lo_ref, tie_ref, *, top_k, past):
    n_chunks, page, db = keys_ref.shape
    n_pages = n_chunks - 1
    for p in range(n_pages):
        _store_keys(keys_ref, hi_ref, lo_ref, p, sc_ref[:, p * page:(p + 1) * page].T)
    kin = kin_ref[...].astype(BF16).astype(F32)
    qif = qi_ref[...].astype(F32)
    w = wi_ref[...]
    s_new = jnp.zeros((db, 1), F32)
    for h in range(IDX_HEADS):
        qh = qif[:, h * IDX_DIM:(h + 1) * IDX_DIM]
        s_new = s_new + w[:, h:h + 1] * jnp.maximum(jnp.sum(qh * kin, axis=-1, keepdims=True), 0.0)
    lane = lax.broadcasted_iota(I32, (db, page), 1)
    _store_keys(keys_ref, hi_ref, lo_ref, n_pages, jnp.where(lane == 0, s_new, NEG_INF).T)
    _park_chunk(hi_ref, lo_ref, n_chunks)

    thr, _ = _select_threshold(keys_ref, hi_ref, lo_ref, n_chunks, top_k, tie_ref)
    tie = tie_ref[...]
    row = lax.broadcasted_iota(I32, (page, db), 0)
    for p in range(n_chunks):
        key_index = p * page + row
        sel = jnp.logical_and(key_index <= past, _selected(keys_ref[p], key_index, thr, tie))
        o_ref[:, p * page:(p + 1) * page] = jnp.where(sel, 0.0, NEG_INF).T


def _sample_select(scores, qi, ki_new, wi, top_k, past, page):
    db = scores.shape[0]
    n_chunks = past // page + 1
    return pl.pallas_call(
        functools.partial(_sselect_kernel, top_k=top_k, past=past),
        out_shape=jax.ShapeDtypeStruct((db, n_chunks * page), F32),
        scratch_shapes=[pltpu.VMEM((n_chunks, page, db), I32),
                        pltpu.VMEM((n_chunks + 1, page, db), I16),
                        pltpu.VMEM((n_chunks + 1, page, db), I16),
                        pltpu.VMEM((1, db), I32)],
        compiler_params=pltpu.CompilerParams(vmem_limit_bytes=VMEM_LIMIT_BYTES),
        name="sample_select",
    )(scores, qi, ki_new, wi)


def _sattn_kernel(pt_ref, q_ref, mask_ref, bias_ref, kn_ref, vn_ref, *refs):
    n_pages = (len(refs) - 1) // 2
    k_refs, v_refs, o_ref = refs[:n_pages], refs[n_pages:2 * n_pages], refs[-1]
    a = q_ref.shape[2]
    page = k_refs[0].shape[3]
    q = q_ref[0]
    lane_head = lax.shift_right_logical(lax.broadcasted_iota(I32, (N_HEADS, a), 1),
                                        HEAD_DIM.bit_length() - 1)
    own = lane_head == lax.broadcasted_iota(I32, (N_HEADS, a), 0)
    qf = q.astype(F32)
    q_heads = jnp.where(own, qf, 0.0).astype(BF16)
    parts = [_dot(q_heads, k_ref[0, 0].astype(BF16)) for k_ref in k_refs]
    kn = kn_ref[0].astype(BF16).astype(F32)
    g_new = jnp.sum(jnp.where(own, qf * kn, 0.0), axis=-1, keepdims=True)
    parts.append(jnp.where(lax.broadcasted_iota(I32, (N_HEADS, page), 1) == 0, g_new, 0.0))
    s = jnp.concatenate(parts, axis=1) + bias_ref[...] + mask_ref[0]
    p = jnp.exp2(s - jnp.max(s, axis=-1, keepdims=True))
    l = jnp.sum(p, axis=-1, keepdims=True)
    new_col = n_pages * page
    acc = p[:, new_col:new_col + 1] * vn_ref[0].astype(BF16).astype(F32)
    for pg, v_ref in enumerate(v_refs):
        acc = acc + _dot_nt(p[:, pg * page:(pg + 1) * page].astype(BF16), v_ref[0, 0].astype(BF16))
    o_ref[0] = jnp.sum(jnp.where(own, acc / l, 0.0), axis=0, keepdims=True).astype(BF16)


def _sample_attention(layer, page_table_flat, q3, mask3, bias_s, k_new3, v_new3, kt_cache, vt_cache, n_pages):
    db, _, a = q3.shape
    page = kt_cache.shape[3]
    n_keys = mask3.shape[2]
    per_sample = lambda b, pt: (b, 0, 0)
    return pl.pallas_call(
        _sattn_kernel,
        grid_spec=pltpu.PrefetchScalarGridSpec(
            num_scalar_prefetch=1,
            grid=(db,),
            in_specs=[pl.BlockSpec((1, 1, a), per_sample),
                      pl.BlockSpec((1, 1, n_keys), per_sample),
                      _resident((N_HEADS, n_keys), lambda b, pt: (0, 0)),
                      pl.BlockSpec((1, 1, a), per_sample),
                      pl.BlockSpec((1, 1, a), per_sample)]
                     + _page_specs(layer, n_pages, a, page) + _page_specs(layer, n_pages, a, page),
            out_specs=pl.BlockSpec((1, 1, a), per_sample)),
        out_shape=jax.ShapeDtypeStruct((db, 1, a), BF16),
        compiler_params=_cparams(1),
        name="sample_attn",
    )(page_table_flat, q3, mask3, bias_s, k_new3, v_new3, *([kt_cache] * n_pages), *([vt_cache] * n_pages))


def _merge_kernel(h_ref, z_ref, zp1_ref, zp2_ref, prev_ref, bc_ref, sgc_ref, sga_ref, o_ref, cw_ref,
                  woc_ref, woa_ref, wout_ref, g_ref, wgu_ref, wd_ref, gf_ref, x_ref, acc_ref, *,
                  tiles_per_seq, sequence, d_ff, final_norm):
    z = z_ref[...]
    if sequence:
        first = (pl.program_id(0) % tiles_per_seq) == 0
        before1 = jnp.where(first, prev_ref[0, 1:2, :], zp1_ref[7:8, :])
        before2 = jnp.where(first, prev_ref[0, 0:1, :], zp1_ref[6:7, :])
        rowi = lax.broadcasted_iota(I32, z.shape, 0)
        z1 = jnp.where(rowi == 0, before1, pltpu.roll(z, 1, 0))
        z2 = jnp.where(rowi == 0, before2, jnp.where(rowi == 1, before1, pltpu.roll(z, 2, 0)))
    else:
        z1 = zp1_ref[...]
        z2 = zp2_ref[...]
    conv = cw_ref[0:1, :] * z2 + cw_ref[1:2, :] * z1 + cw_ref[2:3, :] * z
    y_conv = _dot((bc_ref[...] * conv).astype(BF16), woc_ref[...])
    y_attn = _dot(o_ref[...], woa_ref[...])
    merged = sgc_ref[...] * y_conv + sga_ref[...] * y_attn
    x = h_ref[...] + _dot(merged.astype(BF16), wout_ref[...])
    x_ref[...] = _ffn_rows(x, g_ref, wgu_ref, wd_ref, gf_ref, acc_ref, d_ff, final_norm)


def _merge(h, z, zp1, zp2, prev, bc, sgc, sga, o, cw, woc, woa, wout, g, wgu, wd, g_final, final_norm,
           batch, sequence):
    m, d = h.shape
    d_ff = wd.shape[0]
    c = z.shape[1]
    seq = m // batch
    tm = min(ROW_TILE, seq) if sequence else min(ROW_TILE, m)
    tiles_per_seq = seq // tm if sequence else 1
    row = lambda i: (i, 0)
    const = lambda i: (0, 0)
    if sequence:
        halo = tm // SUBLANES
        zp1_spec = pl.BlockSpec((SUBLANES, c), lambda i: (jnp.maximum(i * halo - 1, 0), 0))
        zp2_spec = pl.BlockSpec((SUBLANES, c), lambda i: (0, 0))
        prev_spec = pl.BlockSpec((1, CONV_W - 1, c), lambda i: (i // tiles_per_seq, 0, 0))
    else:
        zp1_spec = pl.BlockSpec((tm, c), row)
        zp2_spec = pl.BlockSpec((tm, c), row)
        prev_spec = pl.BlockSpec((1, CONV_W - 1, c), lambda i: (0, 0, 0))
    return pl.pallas_call(
        functools.partial(_merge_kernel, tiles_per_seq=tiles_per_seq, sequence=sequence, d_ff=d_ff,
                          final_norm=final_norm),
        grid=(m // tm,),
        in_specs=[pl.BlockSpec((tm, d), row),
                  pl.BlockSpec((tm, c), row),
                  zp1_spec, zp2_spec, prev_spec,
                  pl.BlockSpec((tm, c), row),
                  pl.BlockSpec((tm, d), row),
                  pl.BlockSpec((tm, d), row),
                  pl.BlockSpec((tm, o.shape[1]), row),
                  _resident((CONV_W, c), const),
                  _resident(woc.shape, const),
                  _resident(woa.shape, const),
                  _resident(wout.shape, const),
                  _resident((1, d), const),
                  _resident((d, 2 * d_ff), const),
                  _resident((d_ff, d), const),
                  _resident((1, d), const)],
        out_specs=pl.BlockSpec((tm, d), row),
        out_shape=jax.ShapeDtypeStruct((m, d), F32),
        scratch_shapes=[pltpu.VMEM((tm, d), F32)],
        compiler_params=_cparams(1),
        name="merge_ffn",
    )(h, z, zp1, zp2, prev, bc, sgc, sga, o, cw, woc, woa, wout, g, wgu, wd, g_final)


def _arrange_w_in(w, d):
    c = a = d // 2
    o = 3 * c + 3 * a + IDX_HEADS * IDX_DIM
    head = w[:, :o]
    ki = jnp.pad(w[:, o:o + IDX_DIM], ((0, 0), (0, LANES - IDX_DIM)))
    wi = jnp.pad(w[:, o + IDX_DIM:o + IDX_DIM + IDX_HEADS], ((0, 0), (0, LANES - IDX_HEADS)))
    gates = w[:, o + IDX_DIM + IDX_HEADS:]
    return jnp.concatenate([head, ki, wi, gates], axis=1).astype(BF16)


def kernel(x_prompt, x_sample, cache_k, cache_v, cache_idx_k, state_conv, page_table, rel_bias,
           norm_ffn1, w_ffn1_in, w_ffn1_out, norm_mix, w_in, conv_w, w_o_conv, w_o_attn, w_out,
           norm_ffn2, w_ffn2_in, w_ffn2_out, norm_final):
    batch, seq, d = x_prompt.shape
    db, dec_seq, _ = x_sample.shape
    assert dec_seq == 1
    depth = w_in.shape[0]
    c = a = d // 2
    n_pool, page = cache_k.shape[1:3]
    n_pages = page_table.shape[1]
    past = n_pages * page
    n_keys_s = past + page
    t = min(ATT_TILE, seq)
    assert seq % t == 0 and seq % min(ROW_TILE, seq) == 0 and t >= REL_MAX_DIST
    top_k_p = min(TOPK_MAX, seq // 4)
    top_k_s = min(TOPK_MAX, (past + dec_seq) // 4)

    bias0, bias1, bias_s = _bias_tables(rel_bias, t, past, n_keys_s)
    pt_flat = page_table.reshape(-1)
    g_final = norm_final.reshape(1, d)
    conv_zero = jnp.zeros((batch, CONV_W - 1, c), F32)
    kt_cache = jnp.transpose(cache_k, (0, 1, 3, 4, 2)).reshape(depth, n_pool, a, page)
    vt_cache = jnp.transpose(cache_v, (0, 1, 3, 4, 2)).reshape(depth, n_pool, a, page)
    idx_t = jnp.transpose(cache_idx_k, (0, 1, 3, 2))

    xp = x_prompt.reshape(batch * seq, d)
    xs = x_sample.reshape(db, d)
    outs_p = {"k": [], "v": [], "ik": [], "conv": []}
    outs_s = {"k": [], "v": [], "ik": [], "conv": []}

    for l in range(depth):
        last = l == depth - 1
        g1 = norm_ffn1[l].reshape(1, d)
        g2 = norm_ffn2[l].reshape(1, d)
        gm = norm_mix[l].reshape(1, d)
        w1_in, w1_out = w_ffn1_in[l].astype(BF16), w_ffn1_out[l].astype(BF16)
        w2_in, w2_out = w_ffn2_in[l].astype(BF16), w_ffn2_out[l].astype(BF16)
        wp = _arrange_w_in(w_in[l], d)
        woc, woa, wout = w_o_conv[l].astype(BF16), w_o_attn[l].astype(BF16), w_out[l].astype(BF16)
        cw = conv_w[l]

        hp = _ffn(xp, g1, w1_in, w1_out, g_final, False)
        (z, bc, sgc, sga, kt, vt, kit, kb, ki2, vtb, qt, qit, wit) = _proj(hp, gm, wp, batch, True, t)
        o = _prompt_attention(qt, qit, wit, kb, vtb, ki2, bias0, bias1, top_k_p)
        xp = _merge(hp, z, z, z, conv_zero, bc, sgc, sga, o, cw, woc, woa, wout, g2, w2_in, w2_out, g_final,
                    last, batch, True)
        outs_p["k"].append(kt)
        outs_p["v"].append(vt)
        outs_p["ik"].append(kit)
        outs_p["conv"].append(z.reshape(batch, seq, c)[:, seq - (CONV_W - 1):])

        hs = _ffn(xs, g1, w1_in, w1_out, g_final, False)
        (z, bc, sgc, sga, kt, vt, kit, q, qi, wi, k, v, ki) = _proj(hs, gm, wp, 1, False)
        scores = _sample_scores(l, pt_flat, qi.reshape(db, IDX_HEADS, IDX_DIM),
                                wi.reshape(db, IDX_HEADS, 1), idx_t, n_pages)
        mask = _sample_select(scores.reshape(db, past), qi, ki, wi, top_k_s, past, page)
        o = _sample_attention(l, pt_flat, q.reshape(db, 1, a), mask.reshape(db, 1, n_keys_s), bias_s,
                              k.reshape(db, 1, a), v.reshape(db, 1, a), kt_cache, vt_cache, n_pages)
        prev = state_conv[l]
        xs = _merge(hs, z, prev[:, 1], prev[:, 0], conv_zero, bc, sgc, sga, o.reshape(db, a), cw,
                    woc, woa, wout, g2, w2_in, w2_out, g_final, last, 1, False)
        outs_s["k"].append(kt)
        outs_s["v"].append(vt)
        outs_s["ik"].append(kit)
        outs_s["conv"].append(jnp.stack([prev[:, 1], z], axis=1))

    def heads_last(parts, lead, n_pos):
        x = jnp.stack(parts).reshape(depth, -1, N_HEADS, HEAD_DIM, n_pos)
        return jnp.transpose(x, (0, 1, 4, 2, 3)).reshape((depth,) + lead + (N_HEADS, HEAD_DIM))

    def dim_last(parts, lead, n_pos):
        x = jnp.transpose(jnp.stack(parts), (0, 1, 3, 2))
        return x.reshape((depth,) + lead + (IDX_DIM,))

    return (xp.reshape(batch, seq, d),
            xs.reshape(db, dec_seq, d),
            heads_last(outs_p["k"], (batch, seq), seq),
            heads_last(outs_p["v"], (batch, seq), seq),
            dim_last(outs_p["ik"], (batch, seq), seq),
            jnp.stack(outs_p["conv"]),
            heads_last(outs_s["k"], (db, dec_seq), db),
            heads_last(outs_s["v"], (db, dec_seq), db),
            dim_last(outs_s["ik"], (db, dec_seq), db),
            jnp.stack(outs_s["conv"]))
```
